```python
import math
import jax, jax.numpy as jnp
from jax import lax
import numpy as np


D_MODEL = 1024
BATCH = 8
SEQ = 4096
DEPTH = 2

GRID_W = 64
CTX_LEN = 256
N_ADA = 9
FFN_DIM = 2816
RMS_EPS = 1e-6
ROPE_THETA = 10000.0
Q_BLOCK = 128
MLA_HEADS = 8
MLA_NOPE = 64
MLA_ROPE = 32
MLA_V = 64
MLA_Q_LORA = 256
MLA_KV_LORA = 128
MLA_SCALE = (MLA_NOPE + MLA_ROPE) ** -0.5
GLA_HEADS = 4
GLA_DK = 64
GLA_DV = 128
GLA_GATE_RANK = 16
GLA_GATE_TEMP = 16.0
GLA_CHUNK = 64
GQA_HEADS = 8
GQA_KV_HEADS = 2
GQA_HEAD_DIM = 64
GQA_SCALE = GQA_HEAD_DIM ** -0.5
HY_WIDTH = 512
HY_ORDER = 2
HY_SHORT = 3
HY_BANDS = 16
HY_EMB = 2 * HY_BANDS + 1
HY_HIDDEN = 64
HY_SIN_FREQ = 1.0
HY_DECAY_TARGET = 1e-2
HY_SHORT_DECAY_PCT = 0.3
HY_LONG_DECAY_PCT = 1.5
HY_FILTER_OUT_STD = 0.01
N_BRANCH = 4
BRANCH_WIDTH = 512

IN_SIZES = (MLA_Q_LORA, MLA_KV_LORA, MLA_ROPE,
            GLA_HEADS * GLA_DK, GLA_HEADS * GLA_DK, GLA_HEADS * GLA_DV, 2 * GLA_GATE_RANK, GLA_HEADS * GLA_DV,
            GQA_HEADS * GQA_HEAD_DIM, GQA_KV_HEADS * GQA_HEAD_DIM, GQA_KV_HEADS * GQA_HEAD_DIM,
            (HY_ORDER + 1) * HY_WIDTH,
            N_BRANCH * D_MODEL)
P_IN = sum(IN_SIZES)

kernel_name = "hybrid_parallel_mla_gla_gqa_hyena_dit_block"


def rms_norm(x, g):
    xf = x.astype(jnp.float32)
    y = xf * lax.rsqrt(jnp.mean(xf * xf, axis=-1, keepdims=True) + RMS_EPS)
    return y.astype(x.dtype) * g


def modulate(x, shift, scale):
    return x * (1.0 + scale) + shift


def swiglu(x, w_gate, w_up, w_down):
    return (jax.nn.silu(x @ w_gate) * (x @ w_up)) @ w_down


def split_cols(p):
    offsets, acc = [], 0
    for s in IN_SIZES[:-1]:
        acc += s
        offsets.append(acc)
    return jnp.split(p, offsets, axis=-1)


def grid_positions(rows):
    r = jnp.repeat(jnp.arange(rows, dtype=jnp.float32), GRID_W)
    col = jnp.tile(jnp.arange(GRID_W, dtype=jnp.float32), rows)
    return r, col


def rope_1d(t, pos):
    half = t.shape[-1] // 2
    freqs = ROPE_THETA ** (-jnp.arange(half, dtype=jnp.float32) / half)
    ang = pos[:, None] * freqs[None, :]
    cos, sin = jnp.cos(ang).astype(t.dtype), jnp.sin(ang).astype(t.dtype)
    t1, t2 = t[..., :half], t[..., half:]
    return jnp.concatenate([t1 * cos - t2 * sin, t1 * sin + t2 * cos], axis=-1)


def axial_rope(t, pos):
    rows, cols = pos
    r = t.shape[-1] // 2
    return jnp.concatenate([rope_1d(t[..., :r], rows), rope_1d(t[..., r:], cols)], axis=-1)


def attend(q, k, v, scale):
    B, G, R, Lq, d = q.shape
    nb = Lq // Q_BLOCK
    qb = jnp.moveaxis(q.reshape(B, G, R, nb, Q_BLOCK, d), 3, 0)

    def one_block(qblk):
        s = jnp.einsum('bgrqd,bgkd->bgrqk', qblk, k).astype(jnp.float32) * scale
        p = jax.nn.softmax(s, axis=-1).astype(v.dtype)
        return jnp.einsum('bgrqk,bgkv->bgrqv', p, v)

    o = lax.map(one_block, qb)
    return jnp.moveaxis(o, 0, 3).reshape(B, G, R, Lq, v.shape[-1])


def heads_out(o):
    B, G, R, L, dv = o.shape
    return jnp.transpose(o, (0, 3, 1, 2, 4)).reshape(B, L, G * R * dv)


def mla_qkv(c_q, c_kv, k_rope, q_norm_g, w_uq, kv_norm_g, w_ukv, pos):
    B, L, _ = c_q.shape
    q = (rms_norm(c_q, q_norm_g) @ w_uq).reshape(B, L, MLA_HEADS, MLA_NOPE + MLA_ROPE).transpose(0, 2, 1, 3)
    kv = (rms_norm(c_kv, kv_norm_g) @ w_ukv).reshape(B, L, MLA_HEADS, MLA_NOPE + MLA_V).transpose(0, 2, 1, 3)
    q_nope, q_rope = q[..., :MLA_NOPE], q[..., MLA_NOPE:]
    if pos is not None:
        q_rope = axial_rope(q_rope, pos)
        k_rope = axial_rope(k_rope, pos)
    q = jnp.concatenate([q_nope, q_rope], axis=-1)
    k = jnp.concatenate([kv[..., :MLA_NOPE], jnp.broadcast_to(k_rope[:, None], (B, MLA_HEADS, L, MLA_ROPE))], axis=-1)
    return q[:, :, None], k, kv[..., MLA_NOPE:]


def gqa_qkv(q, k, v, q_norm_g, k_norm_g, pos):
    B, L, _ = q.shape
    q = rms_norm(q.reshape(B, L, GQA_HEADS, GQA_HEAD_DIM), q_norm_g).transpose(0, 2, 1, 3)
    k = rms_norm(k.reshape(B, L, GQA_KV_HEADS, GQA_HEAD_DIM), k_norm_g).transpose(0, 2, 1, 3)
    v = v.reshape(B, L, GQA_KV_HEADS, GQA_HEAD_DIM).transpose(0, 2, 1, 3)
    if pos is not None:
        q = axial_rope(q, pos)
        k = axial_rope(k, pos)
    return q.reshape(B, GQA_KV_HEADS, GQA_HEADS // GQA_KV_HEADS, L, GQA_HEAD_DIM), k, v


def gla_inputs(q, k, v, g_lr, w_gate, b_gate):
    B, L, _ = q.shape

    def heads(t, d):
        return t.reshape(B, L, GLA_HEADS, d).transpose(0, 2, 1, 3)

    lr = g_lr.reshape(B, L, 2, GLA_GATE_RANK)
    pre = jnp.einsum('blnr,nrk->blnk', lr, w_gate) + b_gate
    log_a = jax.nn.log_sigmoid(pre.astype(jnp.float32)) / GLA_GATE_TEMP
    return (heads(q, GLA_DK) * GLA_DK ** -0.5, heads(k, GLA_DK), heads(v, GLA_DV),
            heads(log_a[:, :, 0], GLA_DK), heads(log_a[:, :, 1], GLA_DK))


def gla_chunk_scan(q, k, v, log_a, s0):
    B, H, L, DK = q.shape
    DV = v.shape[-1]
    n = L // GLA_CHUNK

    def chunks(t):
        return jnp.moveaxis(t.reshape(B, H, n, GLA_CHUNK, t.shape[-1]), 2, 0)

    lower = jnp.tril(jnp.ones((GLA_CHUNK, GLA_CHUNK), dtype=bool))

    def step(s, blk):
        qb, kb, vb, ab = blk
        bcum = jnp.cumsum(ab, axis=-2)
        b_last = bcum[:, :, -1:, :]
        q_dec = qb * jnp.exp(bcum)
        k_inv = kb * jnp.exp(-bcum)
        a = jnp.where(lower, jnp.einsum('bhid,bhjd->bhij', q_dec, k_inv), 0.0)
        o = jnp.einsum('bhij,bhjv->bhiv', a, vb) + jnp.einsum('bhid,bhdv->bhiv', q_dec, s)
        k_end = kb * jnp.exp(b_last - bcum)
        s_new = jnp.exp(b_last[:, :, 0, :, None]) * s + jnp.einsum('bhjd,bhjv->bhdv', k_end, vb)
        return s_new, o

    s_fin, o = lax.scan(step, s0, (chunks(q), chunks(k), chunks(v), chunks(log_a)))
    o = jnp.moveaxis(o, 0, 2).reshape(B, H, L, DV)
    return o.astype(v.dtype), s_fin


def gla_bidir(q, k, v, la_f, la_b, s_f0, s_b0):
    o_f, s_f = gla_chunk_scan(q, k, v, la_f, s_f0)
    flip = lambda t: jnp.flip(t, axis=2)
    o_b, s_b = gla_chunk_scan(flip(q), flip(k), flip(v), flip(la_b), s_b0)
    return o_f + flip(o_b), s_f, s_b


def gla_output(o, out_gate, norm_g):
    B, H, L, DV = o.shape
    o = rms_norm(o, norm_g).transpose(0, 2, 1, 3).reshape(B, L, H * DV)
    return o * jax.nn.silu(out_gate)


def hyena_filters(L, w1, b1, w2, b2, w3):
    t = jnp.linspace(0.0, 1.0, L, dtype=jnp.float32)
    w = 2.0 * math.pi * jnp.arange(L, dtype=jnp.float32) / L
    f = jnp.linspace(1e-4, HY_BANDS - 1, HY_BANDS, dtype=jnp.float32)
    fw = w[:, None] * f[None, :]
    feat = jnp.concatenate([t[:, None], jnp.cos(fw), -jnp.sin(fw)], axis=-1)
    h = jnp.sin(HY_SIN_FREQ * (feat @ w1 + b1))
    h = jnp.sin(HY_SIN_FREQ * (h @ w2 + b2))
    h = (h @ w3).reshape(L, HY_ORDER, 2, HY_WIDTH)
    deltas = jnp.abs(jnp.linspace(math.log(HY_DECAY_TARGET) / HY_LONG_DECAY_PCT,
                                  math.log(HY_DECAY_TARGET) / HY_SHORT_DECAY_PCT, HY_WIDTH, dtype=jnp.float32))
    h = h * jnp.exp(-t[:, None] * deltas[None, :])[:, None, None, :]
    full = jnp.concatenate([h[:, :, 0], jnp.zeros((1, HY_ORDER, HY_WIDTH), h.dtype), jnp.flip(h[1:, :, 1], axis=0)], axis=0)
    return jnp.fft.rfft(full.astype(jnp.float32), axis=0)


def fft_long_conv(u, filt_f, bias):
    L = u.shape[1]
    U = jnp.fft.rfft(u.astype(jnp.float32), n=2 * L, axis=1)
    y = jnp.fft.irfft(U * filt_f[None], n=2 * L, axis=1)[:, :L]
    return (y + u * bias).astype(u.dtype)


def hyena_mixer(p, hy):
    sconv_w, sconv_b, w1, b1, w2, b2, w3, fbias = hy
    C = p.shape[-1]
    z = lax.conv_general_dilated(p, sconv_w[:, None, :], window_strides=(1,), padding='SAME',
                                 dimension_numbers=('NWC', 'WIO', 'NWC'), feature_group_count=C) + sconv_b
    x1, x2, v = jnp.split(z, 3, axis=-1)
    filt = hyena_filters(p.shape[1], w1, b1, w2, b2, w3)
    v = x1 * fft_long_conv(v, filt[:, 0], fbias[0])
    v = x2 * fft_long_conv(v, filt[:, 1], fbias[1])
    return v


def merge(outs, gate_cols, w_branch):
    B, L, _ = gate_cols.shape
    g = jax.nn.sigmoid(gate_cols.reshape(B, L, N_BRANCH, D_MODEL))
    y = g[:, :, 0] * (outs[0] @ w_branch[0])
    for i in range(1, N_BRANCH):
        y = y + g[:, :, i] * (outs[i] @ w_branch[i])
    return y


def token_mixers(u, uc, pos, need_ctx, w_in, mla_q_norm_g, mla_w_uq, mla_kv_norm_g, mla_w_ukv,
                 gla_w_gate, gla_b_gate, gla_norm_g, gqa_q_norm_g, gqa_k_norm_g, hy, w_branch, w_out):
    B = u.shape[0]
    (a_cq, a_ckv, a_kr, b_q, b_k, b_v, b_lr, b_og, c_q, c_k, c_v, d_p, gates) = split_cols(u @ w_in)
    (a_cq_c, a_ckv_c, a_kr_c, b_q_c, b_k_c, b_v_c, b_lr_c, b_og_c, c_q_c, c_k_c, c_v_c, d_p_c, gates_c) = split_cols(uc @ w_in)

    qa, ka, va = mla_qkv(a_cq, a_ckv, a_kr, mla_q_norm_g, mla_w_uq, mla_kv_norm_g, mla_w_ukv, pos)
    qa_c, ka_c, va_c = mla_qkv(a_cq_c, a_ckv_c, a_kr_c, mla_q_norm_g, mla_w_uq, mla_kv_norm_g, mla_w_ukv, None)
    o_a = heads_out(attend(qa, jnp.concatenate([ka, ka_c], axis=2), jnp.concatenate([va, va_c], axis=2), MLA_SCALE))

    s0 = jnp.zeros((B, GLA_HEADS, GLA_DK, GLA_DV), jnp.float32)
    ob_c, s_f, s_b = gla_bidir(*gla_inputs(b_q_c, b_k_c, b_v_c, b_lr_c, gla_w_gate, gla_b_gate), s0, s0)
    ob_l, _, _ = gla_bidir(*gla_inputs(b_q, b_k, b_v, b_lr, gla_w_gate, gla_b_gate), s_f, s_b)
    o_b = gla_output(ob_l, b_og, gla_norm_g)

    qc, kc, vc = gqa_qkv(c_q, c_k, c_v, gqa_q_norm_g, gqa_k_norm_g, pos)
    qc_c, kc_c, vc_c = gqa_qkv(c_q_c, c_k_c, c_v_c, gqa_q_norm_g, gqa_k_norm_g, None)
    o_c = heads_out(attend(qc, jnp.concatenate([kc, kc_c], axis=2), jnp.concatenate([vc, vc_c], axis=2), GQA_SCALE))

    o_d = hyena_mixer(d_p, hy)

    y = merge((o_a, o_b, o_c, o_d), gates, w_branch) @ w_out
    if not need_ctx:
        return y, None

    o_ac = heads_out(attend(qa_c, ka_c, va_c, MLA_SCALE))
    o_bc = gla_output(ob_c, b_og_c, gla_norm_g)
    o_cc = heads_out(attend(qc_c, kc_c, vc_c, GQA_SCALE))
    o_dc = hyena_mixer(d_p_c, hy)
    yc = merge((o_ac, o_bc, o_cc, o_dc), gates_c, w_branch) @ w_out
    return y, yc


def setup_inputs(seed: int = 0) -> dict:
    key = jax.random.key(seed)
    ks = jax.random.split(key, 31)
    f32 = jnp.float32

    def nrm(k, shape, std):
        return std * jax.random.normal(k, shape, f32)

    def gain(k, shape):
        return 1.0 + 0.05 * jax.random.normal(k, shape, f32)

    return {
        "x": nrm(ks[0], (BATCH, SEQ, D_MODEL), 1.0),
        "c": nrm(ks[1], (BATCH, D_MODEL), 1.0),
        "ctx": nrm(ks[2], (BATCH, CTX_LEN, D_MODEL), 1.0),
        "c_ctx": nrm(ks[3], (D_MODEL,), 1.0),
        "ada_w": nrm(ks[4], (DEPTH, D_MODEL, N_ADA * D_MODEL), 0.02),
        "ada_b": nrm(ks[5], (DEPTH, N_ADA * D_MODEL), 0.01),
        "norm_g": gain(ks[6], (DEPTH, 3, D_MODEL)),
        "ffn_w_gate": nrm(ks[7], (DEPTH, 2, D_MODEL, FFN_DIM), D_MODEL ** -0.5),
        "ffn_w_up": nrm(ks[8], (DEPTH, 2, D_MODEL, FFN_DIM), D_MODEL ** -0.5),
        "ffn_w_down": nrm(ks[9], (DEPTH, 2, FFN_DIM, D_MODEL), FFN_DIM ** -0.5),
        "w_in": nrm(ks[10], (DEPTH, D_MODEL, P_IN), D_MODEL ** -0.5),
        "mla_q_norm_g": gain(ks[11], (DEPTH, MLA_Q_LORA)),
        "mla_w_uq": nrm(ks[12], (DEPTH, MLA_Q_LORA, MLA_HEADS * (MLA_NOPE + MLA_ROPE)), MLA_Q_LORA ** -0.5),
        "mla_kv_norm_g": gain(ks[13], (DEPTH, MLA_KV_LORA)),
        "mla_w_ukv": nrm(ks[14], (DEPTH, MLA_KV_LORA, MLA_HEADS * (MLA_NOPE + MLA_V)), MLA_KV_LORA ** -0.5),
        "gla_w_gate": nrm(ks[15], (DEPTH, 2, GLA_GATE_RANK, GLA_HEADS * GLA_DK), GLA_GATE_RANK ** -0.5),
        "gla_b_gate": nrm(ks[16], (DEPTH, 2, GLA_HEADS * GLA_DK), 0.1),
        "gla_norm_g": gain(ks[17], (DEPTH, GLA_DV)),
        "gqa_q_norm_g": gain(ks[18], (DEPTH, GQA_HEAD_DIM)),
        "gqa_k_norm_g": gain(ks[19], (DEPTH, GQA_HEAD_DIM)),
        "hy_sconv_w": nrm(ks[20], (DEPTH, HY_SHORT, (HY_ORDER + 1) * HY_WIDTH), HY_SHORT ** -0.5),
        "hy_sconv_b": nrm(ks[21], (DEPTH, (HY_ORDER + 1) * HY_WIDTH), 0.01),
        "hy_filt_w1": nrm(ks[22], (DEPTH, HY_EMB, HY_HIDDEN), HY_EMB ** -0.5),
        "hy_filt_b1": nrm(ks[23], (DEPTH, HY_HIDDEN), 0.1),
        "hy_filt_w2": nrm(ks[24], (DEPTH, HY_HIDDEN, HY_HIDDEN), HY_HIDDEN ** -0.5),
        "hy_filt_b2": nrm(ks[25], (DEPTH, HY_HIDDEN), 0.1),
        "hy_filt_w3": nrm(ks[26], (DEPTH, HY_HIDDEN, HY_ORDER * 2 * HY_WIDTH), HY_FILTER_OUT_STD),
        "hy_filt_bias": nrm(ks[27], (DEPTH, HY_ORDER, HY_WIDTH), 0.1),
        "w_branch": nrm(ks[28], (DEPTH, N_BRANCH, BRANCH_WIDTH, D_MODEL), BRANCH_WIDTH ** -0.5),
        "w_out": nrm(ks[29], (DEPTH, D_MODEL, D_MODEL), D_MODEL ** -0.5),
        "final_g": gain(ks[30], (D_MODEL,)),
    }


def reference(x, c, ctx, c_ctx, ada_w, ada_b, norm_g, ffn_w_gate, ffn_w_up, ffn_w_down, w_in,
              mla_q_norm_g, mla_w_uq, mla_kv_norm_g, mla_w_ukv, gla_w_gate, gla_b_gate, gla_norm_g,
              gqa_q_norm_g, gqa_k_norm_g, hy_sconv_w, hy_sconv_b, hy_filt_w1, hy_filt_b1, hy_filt_w2,
              hy_filt_b2, hy_filt_w3, hy_filt_bias, w_branch, w_out, final_g):
    B, L, _ = x.shape
    rows = L // GRID_W
    pos = grid_positions(rows)
    silu_c = jax.nn.silu(c)
    silu_cc = jax.nn.silu(c_ctx)
    xc = ctx
    for l in range(DEPTH):
        need_ctx = l < DEPTH - 1
        mod = (silu_c @ ada_w[l] + ada_b[l]).reshape(B, 1, N_ADA, D_MODEL)
        modc = (silu_cc @ ada_w[l] + ada_b[l]).reshape(N_ADA, D_MODEL)
        sh1, sc1, g1, sh2, sc2, g2, sh3, sc3, g3 = [mod[:, :, i] for i in range(N_ADA)]
        sh1c, sc1c, g1c, sh2c, sc2c, g2c, sh3c, sc3c, g3c = [modc[i] for i in range(N_ADA)]

        x = x + 0.5 * g1 * swiglu(modulate(rms_norm(x, norm_g[l, 0]), sh1, sc1), ffn_w_gate[l, 0], ffn_w_up[l, 0], ffn_w_down[l, 0])
        xc = xc + 0.5 * g1c * swiglu(modulate(rms_norm(xc, norm_g[l, 0]), sh1c, sc1c), ffn_w_gate[l, 0], ffn_w_up[l, 0], ffn_w_down[l, 0])

        hy = (hy_sconv_w[l], hy_sconv_b[l], hy_filt_w1[l], hy_filt_b1[l], hy_filt_w2[l], hy_filt_b2[l], hy_filt_w3[l], hy_filt_bias[l])
        y, yc = token_mixers(modulate(rms_norm(x, norm_g[l, 1]), sh2, sc2),
                             modulate(rms_norm(xc, norm_g[l, 1]), sh2c, sc2c),
                             pos, need_ctx, w_in[l], mla_q_norm_g[l], mla_w_uq[l], mla_kv_norm_g[l], mla_w_ukv[l],
                             gla_w_gate[l], gla_b_gate[l], gla_norm_g[l], gqa_q_norm_g[l], gqa_k_norm_g[l],
                             hy, w_branch[l], w_out[l])
        x = x + g2 * y

        x = x + 0.5 * g3 * swiglu(modulate(rms_norm(x, norm_g[l, 2]), sh3, sc3), ffn_w_gate[l, 1], ffn_w_up[l, 1], ffn_w_down[l, 1])
        if need_ctx:
            xc = xc + g2c * yc
            xc = xc + 0.5 * g3c * swiglu(modulate(rms_norm(xc, norm_g[l, 2]), sh3c, sc3c), ffn_w_gate[l, 1], ffn_w_up[l, 1], ffn_w_down[l, 1])
    return rms_norm(x, final_g)
```

```python
import functools
import math

import jax
import jax.numpy as jnp
import numpy as np
from jax import lax
from jax.experimental import pallas as pl
from jax.experimental.pallas import tpu as pltpu

F32 = jnp.float32
BF16 = jnp.bfloat16

D_MODEL = 1024
GRID_W = 64
N_ADA = 9
FFN_DIM = 2816
RMS_EPS = 1e-6
ROPE_THETA = 10000.0
MLA_HEADS = 8
MLA_NOPE = 64
MLA_ROPE = 32
MLA_V = 64
MLA_Q_LORA = 256
MLA_KV_LORA = 128
MLA_SCALE = (MLA_NOPE + MLA_ROPE) ** -0.5
MLA_DPAD = 128
GLA_HEADS = 4
GLA_DK = 64
GLA_DV = 128
GLA_GATE_RANK = 16
GLA_GATE_TEMP = 16.0
GLA_CHUNK = 64
GQA_HEADS = 8
GQA_KV_HEADS = 2
GQA_HEAD_DIM = 64
GQA_SCALE = GQA_HEAD_DIM ** -0.5
HY_WIDTH = 512
HY_ORDER = 2
HY_BANDS = 16
HY_SIN_FREQ = 1.0
HY_DECAY_TARGET = 1e-2
HY_SHORT_DECAY_PCT = 0.3
HY_LONG_DECAY_PCT = 1.5
N_BRANCH = 4
BRANCH_WIDTH = 512
LOG2E = math.log2(math.e)

IN_SIZES = (MLA_Q_LORA, MLA_KV_LORA, MLA_ROPE,
            GLA_HEADS * GLA_DK, GLA_HEADS * GLA_DK, GLA_HEADS * GLA_DV, 2 * GLA_GATE_RANK, GLA_HEADS * GLA_DV,
            GQA_HEADS * GQA_HEAD_DIM, GQA_KV_HEADS * GQA_HEAD_DIM, GQA_KV_HEADS * GQA_HEAD_DIM,
            (HY_ORDER + 1) * HY_WIDTH,
            N_BRANCH * D_MODEL)
IN_OFFS = tuple(int(v) for v in np.cumsum((0,) + IN_SIZES))

VMEM_LIMIT_BYTES = 56 * 1024 * 1024


def _cparams(n_axes):
    return pltpu.CompilerParams(dimension_semantics=("arbitrary",) * n_axes,
                                vmem_limit_bytes=VMEM_LIMIT_BYTES)


def _const_spec(shape):
    nd = len(shape)
    return pl.BlockSpec(shape, lambda *_: (0,) * nd, pipeline_mode=pl.Buffered(1))


def _pick_tile(n, pref):
    t = min(n, pref)
    while n % t:
        t //= 2
    return t


def _rms(x):
    return x * lax.rsqrt(jnp.mean(x * x, axis=-1, keepdims=True) + RMS_EPS)


def _norm_mod(x, g, shift, scale):
    return (_rms(x) * g) * (1.0 + scale) + shift


def _ada_kernel(s_ref, w_ref, b_ref, o_ref):
    s = s_ref[...]
    s = s * jax.nn.sigmoid(s)
    o_ref[...] = jnp.dot(s.astype(BF16), w_ref[...].astype(BF16), preferred_element_type=F32) + b_ref[...]


def _ada(cvec, w, b):
    rows, d = cvec.shape
    n = w.shape[1]
    tn = _pick_tile(n, 1152)
    return pl.pallas_call(
        _ada_kernel,
        grid=(n // tn,),
        in_specs=[pl.BlockSpec((rows, d), lambda j: (0, 0)),
                  pl.BlockSpec((d, tn), lambda j: (0, j)),
                  pl.BlockSpec((1, tn), lambda j: (0, j))],
        out_specs=pl.BlockSpec((rows, tn), lambda j: (0, j)),
        out_shape=jax.ShapeDtypeStruct((rows, n), F32),
        compiler_params=_cparams(1),
        name="ada_mod",
    )(cvec, w, b.reshape(1, n))


def _ffn_kernel(x_ref, mod_ref, g_ref, wg_ref, wu_ref, wd_ref, fg_ref, o_ref, *, final_norm):
    x = x_ref[0]
    mod = mod_ref[0]
    h = _norm_mod(x, g_ref[...], mod[0:1], mod[1:2]).astype(BF16)
    a = jnp.dot(h, wg_ref[...], preferred_element_type=F32)
    b = jnp.dot(h, wu_ref[...], preferred_element_type=F32)
    act = (a * jax.nn.sigmoid(a) * b).astype(BF16)
    y = jnp.dot(act, wd_ref[...], preferred_element_type=F32)
    out = x + (0.5 * mod[2:3]) * y
    if final_norm:
        out = _rms(out) * fg_ref[...]
    o_ref[0] = out


def _ffn(x, mod3, g, wg, wu, wd, final_g=None):
    B, L, D = x.shape
    F = wg.shape[1]
    tm = _pick_tile(L, 512)
    final_norm = final_g is not None
    fg = (final_g if final_norm else jnp.ones((D,), F32)).reshape(1, D)
    return pl.pallas_call(
        functools.partial(_ffn_kernel, final_norm=final_norm),
        grid=(B, L // tm),
        in_specs=[pl.BlockSpec((1, tm, D), lambda b, i: (b, i, 0)),
                  pl.BlockSpec((1, 3, D), lambda b, i: (b, 0, 0)),
                  _const_spec((1, D)), _const_spec((D, F)), _const_spec((D, F)), _const_spec((F, D)),
                  _const_spec((1, D))],
        out_specs=pl.BlockSpec((1, tm, D), lambda b, i: (b, i, 0)),
        out_shape=jax.ShapeDtypeStruct((B, L, D), F32),
        compiler_params=_cparams(2),
        name="ffn",
    )(x, mod3, g.reshape(1, D), wg, wu, wd, fg)


QA_SCALE = MLA_SCALE * LOG2E
QC_SCALE = GQA_SCALE * LOG2E
_A_CQ, _A_CKV, _A_KR, _A_KRS, _C_Q, _C_QS, _C_K, _C_KS, _C_V, _ATT_COLS = (
    0, 256, 384, 512, 640, 1152, 1664, 1792, 1920, 2048)


def _attn_prep_kernel(x_ref, mod_ref, g_ref, w_ref, wuq_ref, wukv_ref, gmq_ref, gmkv_ref, gcq_ref, gck_ref,
                      cosa_ref, sina_ref, cosc_ref, sinc_ref,
                      qa_ref, ka_ref, va_ref, qc_ref, kc_ref, vc_ref):
    x = x_ref[0]
    mod = mod_ref[0]
    h = _norm_mod(x, g_ref[...], mod[0:1], mod[1:2]).astype(BF16)
    p = jnp.dot(h, w_ref[...], preferred_element_type=F32)
    cosa, sina = cosa_ref[...], sina_ref[...]
    cosc, sinc = cosc_ref[...], sinc_ref[...]

    kr = p[:, _A_KR:_A_KR + 128] * cosa + p[:, _A_KRS:_A_KRS + 128] * sina
    nq = (_rms(p[:, _A_CQ:_A_CQ + MLA_Q_LORA]) * gmq_ref[...]).astype(BF16)
    q2 = jnp.dot(nq, wuq_ref[...], preferred_element_type=F32)
    nkv = (_rms(p[:, _A_CKV:_A_CKV + MLA_KV_LORA]) * gmkv_ref[...]).astype(BF16)
    kv = jnp.dot(nkv, wukv_ref[...], preferred_element_type=F32)
    hq = MLA_HEADS * MLA_DPAD
    for hd in range(MLA_HEADS):
        lo = hd * MLA_DPAD
        qh = q2[:, lo:lo + MLA_DPAD] * cosa + q2[:, hq + lo:hq + lo + MLA_DPAD] * sina
        qa_ref[0, hd] = (qh * QA_SCALE).astype(BF16)
        ka_ref[0, hd] = (kv[:, lo:lo + MLA_DPAD] + kr).astype(BF16)
        va_ref[0, hd] = kv[:, hq + hd * MLA_V:hq + (hd + 1) * MLA_V].astype(BF16)

    def normed_rot(t, ts, gains):
        inv = lax.rsqrt(jnp.mean(t * t, axis=-1, keepdims=True) + RMS_EPS)
        return ((t * inv) * gains[0:1]) * cosc + ((ts * inv) * gains[1:2]) * sinc

    gq, gk = gcq_ref[...], gck_ref[...]
    dh = GQA_HEAD_DIM
    for hd in range(GQA_HEADS):
        qh = normed_rot(p[:, _C_Q + hd * dh:_C_Q + (hd + 1) * dh], p[:, _C_QS + hd * dh:_C_QS + (hd + 1) * dh], gq)
        qc_ref[0, hd] = (qh * QC_SCALE).astype(BF16)
    for hd in range(GQA_KV_HEADS):
        kh = normed_rot(p[:, _C_K + hd * dh:_C_K + (hd + 1) * dh], p[:, _C_KS + hd * dh:_C_KS + (hd + 1) * dh], gk)
        kc_ref[0, hd] = kh.astype(BF16)
        vc_ref[0, hd] = p[:, _C_V + hd * dh:_C_V + (hd + 1) * dh].astype(BF16)


def _attn_prep(x, mod3, g, wp, tabs):
    B, L, D = x.shape
    tm = _pick_tile(L, 512)
    cosa, sina, cosc, sinc = tabs
    tok = lambda w: pl.BlockSpec((1, tm, w), lambda b, i: (b, i, 0))
    heads = lambda n, w: pl.BlockSpec((1, n, tm, w), lambda b, i: (b, 0, i, 0))
    tab = lambda w: pl.BlockSpec((tm, w), lambda b, i: (i, 0))
    out_shape = (jax.ShapeDtypeStruct((B, MLA_HEADS, L, MLA_DPAD), BF16),
                 jax.ShapeDtypeStruct((B, MLA_HEADS, L, MLA_DPAD), BF16),
                 jax.ShapeDtypeStruct((B, MLA_HEADS, L, MLA_V), BF16),
                 jax.ShapeDtypeStruct((B, GQA_HEADS, L, GQA_HEAD_DIM), BF16),
                 jax.ShapeDtypeStruct((B, GQA_KV_HEADS, L, GQA_HEAD_DIM), BF16),
                 jax.ShapeDtypeStruct((B, GQA_KV_HEADS, L, GQA_HEAD_DIM), BF16))
    return pl.pallas_call(
        _attn_prep_kernel,
        grid=(B, L // tm),
        in_specs=[tok(D), pl.BlockSpec((1, 3, D), lambda b, i: (b, 0, 0)), _const_spec((1, D)),
                  _const_spec(wp["w_att"].shape), _const_spec(wp["wuq2"].shape), _const_spec(wp["wukv2"].shape),
                  _const_spec((1, MLA_Q_LORA)), _const_spec((1, MLA_KV_LORA)),
                  _const_spec((2, GQA_HEAD_DIM)), _const_spec((2, GQA_HEAD_DIM)),
                  tab(MLA_DPAD), tab(MLA_DPAD), tab(GQA_HEAD_DIM), tab(GQA_HEAD_DIM)],
        out_specs=(heads(MLA_HEADS, MLA_DPAD), heads(MLA_HEADS, MLA_DPAD), heads(MLA_HEADS, MLA_V),
                   heads(GQA_HEADS, GQA_HEAD_DIM), heads(GQA_KV_HEADS, GQA_HEAD_DIM),
                   heads(GQA_KV_HEADS, GQA_HEAD_DIM)),
        out_shape=out_shape,
        compiler_params=_cparams(2),
        name="attn_prep",
    )(x, mod3, g.reshape(1, D), wp["w_att"], wp["wuq2"], wp["wukv2"], wp["gmq"], wp["gmkv"], wp["gcq"], wp["gck"],
      cosa, sina, cosc, sinc)


_B_Q, _B_K, _B_V, _B_OG, _B_LR, _D_P, _GATES, _MIX_COLS = (0, 256, 512, 1024, 1536, 1664, 3200, 7296)


def _mix_prep_kernel(x_ref, mod_ref, g_ref, w_ref, wgate_ref, bgate_ref,
                     gq_ref, gk_ref, gv_ref, la_ref, og_ref, dp_ref, gates_ref):
    x = x_ref[0]
    mod = mod_ref[0]
    h = _norm_mod(x, g_ref[...], mod[0:1], mod[1:2]).astype(BF16)
    pg = jnp.dot(h, w_ref[:, _B_Q:_D_P], preferred_element_type=F32)
    gq_ref[0] = pg[:, _B_Q:_B_K] * (GLA_DK ** -0.5)
    gk_ref[0] = pg[:, _B_K:_B_V]
    gv_ref[0] = pg[:, _B_V:_B_OG]
    og_ref[0] = pg[:, _B_OG:_B_LR]
    pre = jnp.dot(pg[:, _B_LR:_D_P].astype(BF16), wgate_ref[...], preferred_element_type=F32) + bgate_ref[...]
    la_ref[0] = (jnp.minimum(pre, 0.0) - jnp.log1p(jnp.exp(-jnp.abs(pre)))) * (1.0 / GLA_GATE_TEMP)
    dp_ref[0] = jnp.dot(h, w_ref[:, _D_P:_GATES], preferred_element_type=F32)
    gates_ref[0] = jax.nn.sigmoid(jnp.dot(h, w_ref[:, _GATES:_MIX_COLS], preferred_element_type=F32)).astype(BF16)


def _mix_prep(x, mod3, g, wp):
    B, L, D = x.shape
    tm = _pick_tile(L, 256)
    tok = lambda w: pl.BlockSpec((1, tm, w), lambda b, i: (b, i, 0))
    widths = (256, 256, 512, 512, 512, 1536, 4096)
    dtypes = (F32, F32, F32, F32, F32, F32, BF16)
    return pl.pallas_call(
        _mix_prep_kernel,
        grid=(B, L // tm),
        in_specs=[tok(D), pl.BlockSpec((1, 3, D), lambda b, i: (b, 0, 0)), _const_spec((1, D)),
                  _const_spec(wp["w_mix"].shape), _const_spec(wp["wgate"].shape), _const_spec((1, 512))],
        out_specs=tuple(tok(w) for w in widths),
        out_shape=tuple(jax.ShapeDtypeStruct((B, L, w), dt) for w, dt in zip(widths, dtypes)),
        compiler_params=_cparams(2),
        name="mix_prep",
    )(x, mod3, g.reshape(1, D), wp["w_mix"], wp["wgate"], wp["bgate"])


def _attn_kernel(q_ref, k_ref, v_ref, o_ref):
    q = q_ref[0, 0]
    s = lax.dot_general(q, k_ref[0, 0], (((1,), (1,)), ((), ())), preferred_element_type=F32)
    m = jnp.max(s, axis=-1, keepdims=True)
    p = jnp.exp2(s - m)
    l = jnp.sum(p, axis=-1, keepdims=True)
    o = jnp.dot(p.astype(BF16), v_ref[0, 0], preferred_element_type=F32)
    o_ref[0, 0] = (o / l).astype(o_ref.dtype)


def _attend(q, k, v, group=1):
    B, H, Lq, d = q.shape
    Lk, dv = v.shape[2], v.shape[3]
    tq = _pick_tile(Lq, 256)
    return pl.pallas_call(
        _attn_kernel,
        grid=(B, H, Lq // tq),
        in_specs=[pl.BlockSpec((1, 1, tq, d), lambda b, h, i: (b, h, i, 0)),
                  pl.BlockSpec((1, 1, Lk, d), lambda b, h, i: (b, h // group, 0, 0)),
                  pl.BlockSpec((1, 1, Lk, dv), lambda b, h, i: (b, h // group, 0, 0))],
        out_specs=pl.BlockSpec((1, 1, tq, dv), lambda b, h, i: (b, h, i, 0)),
        out_shape=jax.ShapeDtypeStruct((B, H, Lq, dv), BF16),
        compiler_params=_cparams(3),
        name="attend",
    )(q, k, v)


def _merge_kernel(x_ref, mod_ref, oa_ref, ob_ref, oc_ref, od_ref, gates_ref, wb_ref, wo_ref, o_ref):
    oa = jnp.concatenate([oa_ref[0, hd] for hd in range(MLA_HEADS)], axis=-1)
    oc = jnp.concatenate([oc_ref[0, hd] for hd in range(GQA_HEADS)], axis=-1)
    branches = (oa, ob_ref[0].astype(BF16), oc, od_ref[0].astype(BF16))
    y = None
    for i, o in enumerate(branches):
        gate = gates_ref[0, :, i * D_MODEL:(i + 1) * D_MODEL].astype(F32)
        t = gate * jnp.dot(o, wb_ref[i], preferred_element_type=F32)
        y = t if y is None else y + t
    z = jnp.dot(y.astype(BF16), wo_ref[...], preferred_element_type=F32)
    o_ref[0] = x_ref[0] + mod_ref[0] * z


def _merge(x, g2, oa, ob, oc, od, gates, wb, wo):
    B, L, D = x.shape
    tm = _pick_tile(L, 512)
    tok = lambda w: pl.BlockSpec((1, tm, w), lambda b, i: (b, i, 0))
    heads = lambda n, w: pl.BlockSpec((1, n, tm, w), lambda b, i: (b, 0, i, 0))
    return pl.pallas_call(
        _merge_kernel,
        grid=(B, L // tm),
        in_specs=[tok(D), pl.BlockSpec((1, 1, D), lambda b, i: (b, 0, 0)),
                  heads(MLA_HEADS, MLA_V), tok(BRANCH_WIDTH), heads(GQA_HEADS, GQA_HEAD_DIM), tok(BRANCH_WIDTH),
                  tok(N_BRANCH * D), _const_spec(wb.shape), _const_spec(wo.shape)],
        out_specs=tok(D),
        out_shape=jax.ShapeDtypeStruct((B, L, D), F32),
        compiler_params=_cparams(2),
        name="merge",
    )(x, g2, oa, ob, oc, od, gates, wb, wo)


def _gla_chunk_scan(q, k, v, log_a, s0):
    B, H, L, DK = q.shape
    DV = v.shape[-1]
    n = L // GLA_CHUNK

    def chunks(t):
        return jnp.moveaxis(t.reshape(B, H, n, GLA_CHUNK, t.shape[-1]), 2, 0)

    lower = jnp.tril(jnp.ones((GLA_CHUNK, GLA_CHUNK), dtype=bool))

    def step(s, blk):
        qb, kb, vb, ab = blk
        bcum = jnp.cumsum(ab, axis=-2)
        b_last = bcum[:, :, -1:, :]
        q_dec = qb * jnp.exp(bcum)
        k_inv = kb * jnp.exp(-bcum)
        a = jnp.where(lower, jnp.einsum('bhid,bhjd->bhij', q_dec, k_inv), 0.0)
        o = jnp.einsum('bhij,bhjv->bhiv', a, vb) + jnp.einsum('bhid,bhdv->bhiv', q_dec, s)
        k_end = kb * jnp.exp(b_last - bcum)
        s_new = jnp.exp(b_last[:, :, 0, :, None]) * s + jnp.einsum('bhjd,bhjv->bhdv', k_end, vb)
        return s_new, o

    s_fin, o = lax.scan(step, s0, (chunks(q), chunks(k), chunks(v), chunks(log_a)))
    o = jnp.moveaxis(o, 0, 2).reshape(B, H, L, DV)
    return o, s_fin


def _gla_bidir(q, k, v, la, s_f0, s_b0):
    B, L, _ = q.shape
    hd = lambda t, d: t.reshape(B, L, GLA_HEADS, d).transpose(0, 2, 1, 3)
    q, k, v = hd(q, GLA_DK), hd(k, GLA_DK), hd(v, GLA_DV)
    la_f, la_b = hd(la[..., :256], GLA_DK), hd(la[..., 256:], GLA_DK)
    o_f, s_f = _gla_chunk_scan(q, k, v, la_f, s_f0)
    flip = lambda t: jnp.flip(t, axis=2)
    o_b, s_b = _gla_chunk_scan(flip(q), flip(k), flip(v), flip(la_b), s_b0)
    return o_f + flip(o_b), s_f, s_b


def _gla_output(o, out_gate, norm_g):
    B, H, L, DV = o.shape
    o = (_rms(o) * norm_g).transpose(0, 2, 1, 3).reshape(B, L, H * DV)
    return o * jax.nn.silu(out_gate)


def _hyena_filters(L, w1, b1, w2, b2, w3):
    t = jnp.linspace(0.0, 1.0, L, dtype=F32)
    w = 2.0 * math.pi * jnp.arange(L, dtype=F32) / L
    f = jnp.linspace(1e-4, HY_BANDS - 1, HY_BANDS, dtype=F32)
    fw = w[:, None] * f[None, :]
    feat = jnp.concatenate([t[:, None], jnp.cos(fw), -jnp.sin(fw)], axis=-1)
    h = jnp.sin(HY_SIN_FREQ * (feat @ w1 + b1))
    h = jnp.sin(HY_SIN_FREQ * (h @ w2 + b2))
    h = (h @ w3).reshape(L, HY_ORDER, 2, HY_WIDTH)
    deltas = jnp.abs(jnp.linspace(math.log(HY_DECAY_TARGET) / HY_LONG_DECAY_PCT,
                                  math.log(HY_DECAY_TARGET) / HY_SHORT_DECAY_PCT, HY_WIDTH, dtype=F32))
    h = h * jnp.exp(-t[:, None] * deltas[None, :])[:, None, None, :]
    full = jnp.concatenate([h[:, :, 0], jnp.zeros((1, HY_ORDER, HY_WIDTH), h.dtype), jnp.flip(h[1:, :, 1], axis=0)], axis=0)
    return jnp.fft.rfft(full.astype(F32), axis=0)


def _fft_long_conv(u, filt_f, bias):
    L = u.shape[1]
    U = jnp.fft.rfft(u.astype(F32), n=2 * L, axis=1)
    y = jnp.fft.irfft(U * filt_f[None], n=2 * L, axis=1)[:, :L]
    return y + u * bias


def _hyena_mixer(p, hy):
    sconv_w, sconv_b, w1, b1, w2, b2, w3, fbias = hy
    C = p.shape[-1]
    z = lax.conv_general_dilated(p, sconv_w[:, None, :], window_strides=(1,), padding='SAME',
                                 dimension_numbers=('NWC', 'WIO', 'NWC'), feature_group_count=C) + sconv_b
    x1, x2, v = jnp.split(z, 3, axis=-1)
    filt = _hyena_filters(p.shape[1], w1, b1, w2, b2, w3)
    v = x1 * _fft_long_conv(v, filt[:, 0], fbias[0])
    v = x2 * _fft_long_conv(v, filt[:, 1], fbias[1])
    return v


def _swap_perm(width):
    q = width // 4
    return np.concatenate([np.arange(q, 2 * q), np.arange(0, q), np.arange(3 * q, 4 * q), np.arange(2 * q, 3 * q)])


def _rope_tables(L, width, latent):
    if not latent:
        return jnp.ones((L, width), F32), jnp.zeros((L, width), F32)
    q = width // 4
    t = jnp.arange(L, dtype=jnp.int32)
    row = (t // GRID_W).astype(F32)
    col = (t % GRID_W).astype(F32)
    freqs = ROPE_THETA ** (-jnp.arange(q, dtype=F32) / q)
    ar, ac = row[:, None] * freqs[None, :], col[:, None] * freqs[None, :]
    cos = jnp.concatenate([jnp.cos(ar), jnp.cos(ar), jnp.cos(ac), jnp.cos(ac)], axis=-1)
    sin = jnp.concatenate([-jnp.sin(ar), jnp.sin(ar), -jnp.sin(ac), jnp.sin(ac)], axis=-1)
    return cos, sin


def _tables(L, latent):
    ca, sa = _rope_tables(L, MLA_ROPE, latent)
    pad = lambda t, fill: jnp.concatenate(
        [jnp.full((L, MLA_NOPE), fill, F32), t, jnp.zeros((L, MLA_DPAD - MLA_NOPE - MLA_ROPE), F32)], axis=-1)
    cc, sc = _rope_tables(L, GQA_HEAD_DIM, latent)
    return pad(ca, 1.0), pad(sa, 0.0), cc, sc


def _prep_layer_weights(w_in, w_uq, w_ukv, gmq, gmkv, gla_w_gate, gla_b_gate, gcq, gck):
    D = w_in.shape[0]
    col = lambda i: w_in[:, IN_OFFS[i]:IN_OFFS[i + 1]]
    a_cq, a_ckv, a_kr, b_q, b_k, b_v, b_lr, b_og, c_q, c_k, c_v, d_p, gates = [col(i) for i in range(13)]
    p32, p64 = _swap_perm(MLA_ROPE), _swap_perm(GQA_HEAD_DIM)
    z = lambda n: jnp.zeros((D, n), F32)
    pad_kr = lambda t: jnp.concatenate([z(MLA_NOPE), t, z(MLA_DPAD - MLA_NOPE - MLA_ROPE)], axis=1)
    swap_heads = lambda t, n: t.reshape(D, n, GQA_HEAD_DIM)[:, :, p64].reshape(D, n * GQA_HEAD_DIM)
    w_att = jnp.concatenate([a_cq, a_ckv, pad_kr(a_kr), pad_kr(a_kr[:, p32]),
                             c_q, swap_heads(c_q, GQA_HEADS), c_k, swap_heads(c_k, GQA_KV_HEADS), c_v], axis=1)
    w_mix = jnp.concatenate([b_q, b_k, b_v, b_og, b_lr, z(128 - 2 * GLA_GATE_RANK), d_p, gates], axis=1)

    uq = w_uq.reshape(MLA_Q_LORA, MLA_HEADS, MLA_NOPE + MLA_ROPE)
    zq = jnp.zeros((MLA_Q_LORA, MLA_HEADS, MLA_DPAD - MLA_NOPE - MLA_ROPE), F32)
    uq_pad = jnp.concatenate([uq, zq], axis=-1).reshape(MLA_Q_LORA, MLA_HEADS * MLA_DPAD)
    uq_sw = jnp.concatenate([uq[..., :MLA_NOPE], uq[..., MLA_NOPE:][..., p32], zq], axis=-1)
    wuq2 = jnp.concatenate([uq_pad, uq_sw.reshape(MLA_Q_LORA, MLA_HEADS * MLA_DPAD)], axis=1)
    ukv = w_ukv.reshape(MLA_KV_LORA, MLA_HEADS, MLA_NOPE + MLA_V)
    zk = jnp.zeros((MLA_KV_LORA, MLA_HEADS, MLA_DPAD - MLA_NOPE), F32)
    uk = jnp.concatenate([ukv[..., :MLA_NOPE], zk], axis=-1).reshape(MLA_KV_LORA, MLA_HEADS * MLA_DPAD)
    uv = ukv[..., MLA_NOPE:].reshape(MLA_KV_LORA, MLA_HEADS * MLA_V)
    wukv2 = jnp.concatenate([uk, uv], axis=1)

    hk = GLA_HEADS * GLA_DK
    wgate = jnp.zeros((128, 2 * hk), F32)
    wgate = wgate.at[:GLA_GATE_RANK, :hk].set(gla_w_gate[0]).at[GLA_GATE_RANK:2 * GLA_GATE_RANK, hk:].set(gla_w_gate[1])
    return dict(w_att=w_att.astype(BF16), w_mix=w_mix.astype(BF16), wuq2=wuq2.astype(BF16), wukv2=wukv2.astype(BF16),
                gmq=gmq.reshape(1, -1), gmkv=gmkv.reshape(1, -1),
                gcq=jnp.stack([gcq, gcq[p64]]), gck=jnp.stack([gck, gck[p64]]),
                wgate=wgate.astype(BF16), bgate=gla_b_gate.reshape(1, 2 * hk))


def kernel(x, c, ctx, c_ctx, ada_w, ada_b, norm_g, ffn_w_gate, ffn_w_up, ffn_w_down, w_in, mla_q_norm_g, mla_w_uq, mla_kv_norm_g, mla_w_ukv, gla_w_gate, gla_b_gate, gla_norm_g, gqa_q_norm_g, gqa_k_norm_g, hy_sconv_w, hy_sconv_b, hy_filt_w1, hy_filt_b1, hy_filt_w2, hy_filt_b2, hy_filt_w3, hy_filt_bias, w_branch, w_out, final_g):
    B, L, D = x.shape
    Lc = ctx.shape[1]
    depth = ada_w.shape[0]
    tabs_lat = _tables(L, True)
    tabs_ctx = _tables(Lc, False)
    cvec = jnp.concatenate([c, c_ctx[None]], axis=0)
    xc = ctx
    for l in range(depth):
        need_ctx = l < depth - 1
        mod = _ada(cvec, ada_w[l], ada_b[l]).reshape(B + 1, N_ADA, D)
        mod_lat = mod[:B]
        mod_ctx = jnp.broadcast_to(mod[B:], (B, N_ADA, D))
        wg, wu, wd = ffn_w_gate[l].astype(BF16), ffn_w_up[l].astype(BF16), ffn_w_down[l].astype(BF16)
        wp = _prep_layer_weights(w_in[l], mla_w_uq[l], mla_w_ukv[l], mla_q_norm_g[l], mla_kv_norm_g[l],
                                 gla_w_gate[l], gla_b_gate[l], gqa_q_norm_g[l], gqa_k_norm_g[l])
        wb, wo = w_branch[l].astype(BF16), w_out[l].astype(BF16)

        x = _ffn(x, mod_lat[:, 0:3], norm_g[l, 0], wg[0], wu[0], wd[0])
        xc = _ffn(xc, mod_ctx[:, 0:3], norm_g[l, 0], wg[0], wu[0], wd[0])

        qa, ka, va, qc, kc, vc = _attn_prep(x, mod_lat[:, 3:6], norm_g[l, 1], wp, tabs_lat)
        qa_c, ka_c, va_c, qc_c, kc_c, vc_c = _attn_prep(xc, mod_ctx[:, 3:6], norm_g[l, 1], wp, tabs_ctx)
        gq, gk, gv, la, og, dp, gates = _mix_prep(x, mod_lat[:, 3:6], norm_g[l, 1], wp)
        gq_c, gk_c, gv_c, la_c, og_c, dp_c, gates_c = _mix_prep(xc, mod_ctx[:, 3:6], norm_g[l, 1], wp)

        cat = lambda a, b: jnp.concatenate([a, b], axis=2)
        o_a = _attend(qa, cat(ka, ka_c), cat(va, va_c))
        qc_g = qc.reshape(B, GQA_KV_HEADS, (GQA_HEADS // GQA_KV_HEADS) * L, GQA_HEAD_DIM)
        o_c = _attend(qc_g, cat(kc, kc_c), cat(vc, vc_c)).reshape(B, GQA_HEADS, L, GQA_HEAD_DIM)

        s0 = jnp.zeros((B, GLA_HEADS, GLA_DK, GLA_DV), F32)
        ob_c, s_f, s_b = _gla_bidir(gq_c, gk_c, gv_c, la_c, s0, s0)
        ob_l, _, _ = _gla_bidir(gq, gk, gv, la, s_f, s_b)
        o_b = _gla_output(ob_l, og, gla_norm_g[l])

        hy = (hy_sconv_w[l], hy_sconv_b[l], hy_filt_w1[l], hy_filt_b1[l], hy_filt_w2[l], hy_filt_b2[l],
              hy_filt_w3[l], hy_filt_bias[l])
        o_d = _hyena_mixer(dp, hy)

        x = _merge(x, mod_lat[:, 5:6], o_a, o_b, o_c, o_d, gates, wb, wo)
        last = l == depth - 1
        x = _ffn(x, mod_lat[:, 6:9], norm_g[l, 2], wg[1], wu[1], wd[1], final_g=final_g if last else None)
        if need_ctx:
            o_ac = _attend(qa_c, ka_c, va_c)
            qcc_g = qc_c.reshape(B, GQA_KV_HEADS, (GQA_HEADS // GQA_KV_HEADS) * Lc, GQA_HEAD_DIM)
            o_cc = _attend(qcc_g, kc_c, vc_c).reshape(B, GQA_HEADS, Lc, GQA_HEAD_DIM)
            o_bc = _gla_output(ob_c, og_c, gla_norm_g[l])
            o_dc = _hyena_mixer(dp_c, hy)
            xc = _merge(xc, mod_ctx[:, 5:6], o_ac, o_bc, o_cc, o_dc, gates_c, wb, wo)
            xc = _ffn(xc, mod_ctx[:, 6:9], norm_g[l, 2], wg[1], wu[1], wd[1])
    return x
```

```python
import functools
import math

import jax
import jax.numpy as jnp
import numpy as np
from jax import lax
from jax.experimental import pallas as pl
from jax.experimental.pallas import tpu as pltpu

F32 = jnp.float32
BF16 = jnp.bfloat16

D_MODEL = 1024
GRID_W = 64
N_ADA = 9
FFN_DIM = 2816
RMS_EPS = 1e-6
ROPE_THETA = 10000.0
MLA_HEADS = 8
MLA_NOPE = 64
MLA_ROPE = 32
MLA_V = 64
MLA_Q_LORA = 256
MLA_KV_LORA = 128
MLA_SCALE = (MLA_NOPE + MLA_ROPE) ** -0.5
MLA_DPAD = 128
GLA_HEADS = 4
GLA_DK = 64
GLA_DV = 128
GLA_GATE_RANK = 16
GLA_GATE_TEMP = 16.0
GLA_CHUNK = 64
GQA_HEADS = 8
GQA_KV_HEADS = 2
GQA_HEAD_DIM = 64
GQA_SCALE = GQA_HEAD_DIM ** -0.5
HY_WIDTH = 512
HY_ORDER = 2
HY_BANDS = 16
HY_SIN_FREQ = 1.0
HY_DECAY_TARGET = 1e-2
HY_SHORT_DECAY_PCT = 0.3
HY_LONG_DECAY_PCT = 1.5
N_BRANCH = 4
BRANCH_WIDTH = 512
LOG2E = math.log2(math.e)

IN_SIZES = (MLA_Q_LORA, MLA_KV_LORA, MLA_ROPE,
            GLA_HEADS * GLA_DK, GLA_HEADS * GLA_DK, GLA_HEADS * GLA_DV, 2 * GLA_GATE_RANK, GLA_HEADS * GLA_DV,
            GQA_HEADS * GQA_HEAD_DIM, GQA_KV_HEADS * GQA_HEAD_DIM, GQA_KV_HEADS * GQA_HEAD_DIM,
            (HY_ORDER + 1) * HY_WIDTH,
            N_BRANCH * D_MODEL)
IN_OFFS = tuple(int(v) for v in np.cumsum((0,) + IN_SIZES))

VMEM_LIMIT_BYTES = 56 * 1024 * 1024


def _cparams(n_axes):
    return pltpu.CompilerParams(dimension_semantics=("arbitrary",) * n_axes,
                                vmem_limit_bytes=VMEM_LIMIT_BYTES)


def _const_spec(shape):
    nd = len(shape)
    return pl.BlockSpec(shape, lambda *_: (0,) * nd, pipeline_mode=pl.Buffered(1))


def _pick_tile(n, pref):
    t = min(n, pref)
    while n % t:
        t //= 2
    return t


def _rms(x):
    return x * lax.rsqrt(jnp.mean(x * x, axis=-1, keepdims=True) + RMS_EPS)


def _norm_mod(x, g, shift, scale):
    return (_rms(x) * g) * (1.0 + scale) + shift


def _ada_kernel(s_ref, w_ref, b_ref, o_ref):
    s = s_ref[...]
    s = s * jax.nn.sigmoid(s)
    o_ref[...] = jnp.dot(s.astype(BF16), w_ref[...].astype(BF16), preferred_element_type=F32) + b_ref[...]


def _ada(cvec, w, b):
    rows, d = cvec.shape
    n = w.shape[1]
    tn = _pick_tile(n, 1152)
    return pl.pallas_call(
        _ada_kernel,
        grid=(n // tn,),
        in_specs=[pl.BlockSpec((rows, d), lambda j: (0, 0)),
                  pl.BlockSpec((d, tn), lambda j: (0, j)),
                  pl.BlockSpec((1, tn), lambda j: (0, j))],
        out_specs=pl.BlockSpec((rows, tn), lambda j: (0, j)),
        out_shape=jax.ShapeDtypeStruct((rows, n), F32),
        compiler_params=_cparams(1),
        name="ada_mod",
    )(cvec, w, b.reshape(1, n))


def _ffn_kernel(x_ref, mod_ref, g_ref, wg_ref, wu_ref, wd_ref, fg_ref, o_ref, *, final_norm):
    x = x_ref[0]
    mod = mod_ref[0]
    h = _norm_mod(x, g_ref[...], mod[0:1], mod[1:2]).astype(BF16)
    a = jnp.dot(h, wg_ref[...], preferred_element_type=F32)
    b = jnp.dot(h, wu_ref[...], preferred_element_type=F32)
    act = (a * jax.nn.sigmoid(a) * b).astype(BF16)
    y = jnp.dot(act, wd_ref[...], preferred_element_type=F32)
    out = x + (0.5 * mod[2:3]) * y
    if final_norm:
        out = _rms(out) * fg_ref[...]
    o_ref[0] = out


def _ffn(x, mod3, g, wg, wu, wd, final_g=None):
    B, L, D = x.shape
    F = wg.shape[1]
    tm = _pick_tile(L, 512)
    final_norm = final_g is not None
    fg = (final_g if final_norm else jnp.ones((D,), F32)).reshape(1, D)
    return pl.pallas_call(
        functools.partial(_ffn_kernel, final_norm=final_norm),
        grid=(B, L // tm),
        in_specs=[pl.BlockSpec((1, tm, D), lambda b, i: (b, i, 0)),
                  pl.BlockSpec((1, 3, D), lambda b, i: (b, 0, 0)),
                  _const_spec((1, D)), _const_spec((D, F)), _const_spec((D, F)), _const_spec((F, D)),
                  _const_spec((1, D))],
        out_specs=pl.BlockSpec((1, tm, D), lambda b, i: (b, i, 0)),
        out_shape=jax.ShapeDtypeStruct((B, L, D), F32),
        compiler_params=_cparams(2),
        name="ffn",
    )(x, mod3, g.reshape(1, D), wg, wu, wd, fg)


QA_SCALE = MLA_SCALE * LOG2E
QC_SCALE = GQA_SCALE * LOG2E
_A_CQ, _A_CKV, _A_KR, _A_KRS, _C_Q, _C_QS, _C_K, _C_KS, _C_V, _ATT_COLS = (
    0, 256, 384, 512, 640, 1152, 1664, 1792, 1920, 2048)


def _attn_prep_kernel(x_ref, mod_ref, g_ref, w_ref, wuq_ref, wukv_ref, gmq_ref, gmkv_ref, gcq_ref, gck_ref,
                      cosa_ref, sina_ref, cosc_ref, sinc_ref,
                      qa_ref, ka_ref, va_ref, qc_ref, kc_ref, vc_ref):
    x = x_ref[0]
    mod = mod_ref[0]
    h = _norm_mod(x, g_ref[...], mod[0:1], mod[1:2]).astype(BF16)
    p = jnp.dot(h, w_ref[...], preferred_element_type=F32)
    cosa, sina = cosa_ref[...], sina_ref[...]
    cosc, sinc = cosc_ref[...], sinc_ref[...]

    kr = p[:, _A_KR:_A_KR + 128] * cosa + p[:, _A_KRS:_A_KRS + 128] * sina
    nq = (_rms(p[:, _A_CQ:_A_CQ + MLA_Q_LORA]) * gmq_ref[...]).astype(BF16)
    q2 = jnp.dot(nq, wuq_ref[...], preferred_element_type=F32)
    nkv = (_rms(p[:, _A_CKV:_A_CKV + MLA_KV_LORA]) * gmkv_ref[...]).astype(BF16)
    kv = jnp.dot(nkv, wukv_ref[...], preferred_element_type=F32)
    hq = MLA_HEADS * MLA_DPAD
    for hd in range(MLA_HEADS):
        lo = hd * MLA_DPAD
        qh = q2[:, lo:lo + MLA_DPAD] * cosa + q2[:, hq + lo:hq + lo + MLA_DPAD] * sina
        qa_ref[0, hd] = (qh * QA_SCALE).astype(BF16)
        ka_ref[0, hd] = (kv[:, lo:lo + MLA_DPAD] + kr).astype(BF16)
        va_ref[0, hd] = kv[:, hq + hd * MLA_V:hq + (hd + 1) * MLA_V].astype(BF16)

    def normed_rot(t, ts, gains):
        inv = lax.rsqrt(jnp.mean(t * t, axis=-1, keepdims=True) + RMS_EPS)
        return ((t * inv) * gains[0:1]) * cosc + ((ts * inv) * gains[1:2]) * sinc

    gq, gk = gcq_ref[...], gck_ref[...]
    dh = GQA_HEAD_DIM
    for hd in range(GQA_HEADS):
        qh = normed_rot(p[:, _C_Q + hd * dh:_C_Q + (hd + 1) * dh], p[:, _C_QS + hd * dh:_C_QS + (hd + 1) * dh], gq)
        qc_ref[0, hd] = (qh * QC_SCALE).astype(BF16)
    for hd in range(GQA_KV_HEADS):
        kh = normed_rot(p[:, _C_K + hd * dh:_C_K + (hd + 1) * dh], p[:, _C_KS + hd * dh:_C_KS + (hd + 1) * dh], gk)
        kc_ref[0, hd] = kh.astype(BF16)
        vc_ref[0, hd] = p[:, _C_V + hd * dh:_C_V + (hd + 1) * dh].astype(BF16)


def _attn_prep(x, mod3, g, wp, tabs):
    B, L, D = x.shape
    tm = _pick_tile(L, 512)
    cosa, sina, cosc, sinc = tabs
    tok = lambda w: pl.BlockSpec((1, tm, w), lambda b, i: (b, i, 0))
    heads = lambda n, w: pl.BlockSpec((1, n, tm, w), lambda b, i: (b, 0, i, 0))
    tab = lambda w: pl.BlockSpec((tm, w), lambda b, i: (i, 0))
    out_shape = (jax.ShapeDtypeStruct((B, MLA_HEADS, L, MLA_DPAD), BF16),
                 jax.ShapeDtypeStruct((B, MLA_HEADS, L, MLA_DPAD), BF16),
                 jax.ShapeDtypeStruct((B, MLA_HEADS, L, MLA_V), BF16),
                 jax.ShapeDtypeStruct((B, GQA_HEADS, L, GQA_HEAD_DIM), BF16),
                 jax.ShapeDtypeStruct((B, GQA_KV_HEADS, L, GQA_HEAD_DIM), BF16),
                 jax.ShapeDtypeStruct((B, GQA_KV_HEADS, L, GQA_HEAD_DIM), BF16))
    return pl.pallas_call(
        _attn_prep_kernel,
        grid=(B, L // tm),
        in_specs=[tok(D), pl.BlockSpec((1, 3, D), lambda b, i: (b, 0, 0)), _const_spec((1, D)),
                  _const_spec(wp["w_att"].shape), _const_spec(wp["wuq2"].shape), _const_spec(wp["wukv2"].shape),
                  _const_spec((1, MLA_Q_LORA)), _const_spec((1, MLA_KV_LORA)),
                  _const_spec((2, GQA_HEAD_DIM)), _const_spec((2, GQA_HEAD_DIM)),
                  tab(MLA_DPAD), tab(MLA_DPAD), tab(GQA_HEAD_DIM), tab(GQA_HEAD_DIM)],
        out_specs=(heads(MLA_HEADS, MLA_DPAD), heads(MLA_HEADS, MLA_DPAD), heads(MLA_HEADS, MLA_V),
                   heads(GQA_HEADS, GQA_HEAD_DIM), heads(GQA_KV_HEADS, GQA_HEAD_DIM),
                   heads(GQA_KV_HEADS, GQA_HEAD_DIM)),
        out_shape=out_shape,
        compiler_params=_cparams(2),
        name="attn_prep",
    )(x, mod3, g.reshape(1, D), wp["w_att"], wp["wuq2"], wp["wukv2"], wp["gmq"], wp["gmkv"], wp["gcq"], wp["gck"],
      cosa, sina, cosc, sinc)


_B_Q, _B_K, _B_V, _B_OG, _B_LR, _D_P, _GATES, _MIX_COLS = (0, 256, 512, 1024, 1536, 1664, 3200, 7296)


def _mix_prep_kernel(x_ref, mod_ref, g_ref, w_ref, wgate_ref, bgate_ref,
                     gq_ref, gk_ref, gv_ref, la_ref, og_ref, dp_ref, gates_ref):
    x = x_ref[0]
    mod = mod_ref[0]
    h = _norm_mod(x, g_ref[...], mod[0:1], mod[1:2]).astype(BF16)
    pg = jnp.dot(h, w_ref[:, _B_Q:_D_P], preferred_element_type=F32)
    gq_ref[0] = pg[:, _B_Q:_B_K] * (GLA_DK ** -0.5)
    gk_ref[0] = pg[:, _B_K:_B_V]
    gv_ref[0] = pg[:, _B_V:_B_OG]
    og_ref[0] = pg[:, _B_OG:_B_LR]
    pre = jnp.dot(pg[:, _B_LR:_D_P].astype(BF16), wgate_ref[...], preferred_element_type=F32) + bgate_ref[...]
    la_ref[0] = (jnp.minimum(pre, 0.0) - jnp.log1p(jnp.exp(-jnp.abs(pre)))) * (1.0 / GLA_GATE_TEMP)
    dp_ref[0] = jnp.dot(h, w_ref[:, _D_P:_GATES], preferred_element_type=F32)
    gates_ref[0] = jax.nn.sigmoid(jnp.dot(h, w_ref[:, _GATES:_MIX_COLS], preferred_element_type=F32)).astype(BF16)


def _mix_prep(x, mod3, g, wp):
    B, L, D = x.shape
    tm = _pick_tile(L, 256)
    tok = lambda w: pl.BlockSpec((1, tm, w), lambda b, i: (b, i, 0))
    widths = (256, 256, 512, 512, 512, 1536, 4096)
    dtypes = (F32, F32, F32, F32, F32, F32, BF16)
    return pl.pallas_call(
        _mix_prep_kernel,
        grid=(B, L // tm),
        in_specs=[tok(D), pl.BlockSpec((1, 3, D), lambda b, i: (b, 0, 0)), _const_spec((1, D)),
                  _const_spec(wp["w_mix"].shape), _const_spec(wp["wgate"].shape), _const_spec((1, 512))],
        out_specs=tuple(tok(w) for w in widths),
        out_shape=tuple(jax.ShapeDtypeStruct((B, L, w), dt) for w, dt in zip(widths, dtypes)),
        compiler_params=_cparams(2),
        name="mix_prep",
    )(x, mod3, g.reshape(1, D), wp["w_mix"], wp["wgate"], wp["bgate"])


def _attn_kernel(q_ref, k_ref, v_ref, o_ref):
    q = q_ref[0, 0]
    s = lax.dot_general(q, k_ref[0, 0], (((1,), (1,)), ((), ())), preferred_element_type=F32)
    m = jnp.max(s, axis=-1, keepdims=True)
    p = jnp.exp2(s - m)
    l = jnp.sum(p, axis=-1, keepdims=True)
    o = jnp.dot(p.astype(BF16), v_ref[0, 0], preferred_element_type=F32)
    o_ref[0, 0] = (o / l).astype(o_ref.dtype)


def _attend(q, k, v, group=1):
    B, H, Lq, d = q.shape
    Lk, dv = v.shape[2], v.shape[3]
    tq = _pick_tile(Lq, 256)
    return pl.pallas_call(
        _attn_kernel,
        grid=(B, H, Lq // tq),
        in_specs=[pl.BlockSpec((1, 1, tq, d), lambda b, h, i: (b, h, i, 0)),
                  pl.BlockSpec((1, 1, Lk, d), lambda b, h, i: (b, h // group, 0, 0)),
                  pl.BlockSpec((1, 1, Lk, dv), lambda b, h, i: (b, h // group, 0, 0))],
        out_specs=pl.BlockSpec((1, 1, tq, dv), lambda b, h, i: (b, h, i, 0)),
        out_shape=jax.ShapeDtypeStruct((B, H, Lq, dv), BF16),
        compiler_params=_cparams(3),
        name="attend",
    )(q, k, v)


def _merge_kernel(x_ref, mod_ref, oa_ref, ob_ref, og_ref, gn_ref, oc_ref, od_ref, gates_ref, wb_ref, wo_ref, o_ref):
    oa = jnp.concatenate([oa_ref[0, hd] for hd in range(MLA_HEADS)], axis=-1)
    oc = jnp.concatenate([oc_ref[0, hd] for hd in range(GQA_HEADS)], axis=-1)
    ob = ob_ref[0]
    ob = jnp.concatenate([_rms(ob[:, hd * GLA_DV:(hd + 1) * GLA_DV]) * gn_ref[...] for hd in range(GLA_HEADS)], axis=-1)
    og = og_ref[0]
    ob = ob * (og * jax.nn.sigmoid(og))
    branches = (oa, ob.astype(BF16), oc, od_ref[0].astype(BF16))
    y = None
    for i, o in enumerate(branches):
        gate = gates_ref[0, :, i * D_MODEL:(i + 1) * D_MODEL].astype(F32)
        t = gate * jnp.dot(o, wb_ref[i], preferred_element_type=F32)
        y = t if y is None else y + t
    z = jnp.dot(y.astype(BF16), wo_ref[...], preferred_element_type=F32)
    o_ref[0] = x_ref[0] + mod_ref[0] * z


def _merge(x, g2, oa, ob, og, gla_g, oc, od, gates, wb, wo):
    B, L, D = x.shape
    tm = _pick_tile(L, 512)
    tok = lambda w: pl.BlockSpec((1, tm, w), lambda b, i: (b, i, 0))
    heads = lambda n, w: pl.BlockSpec((1, n, tm, w), lambda b, i: (b, 0, i, 0))
    return pl.pallas_call(
        _merge_kernel,
        grid=(B, L // tm),
        in_specs=[tok(D), pl.BlockSpec((1, 1, D), lambda b, i: (b, 0, 0)),
                  heads(MLA_HEADS, MLA_V), tok(BRANCH_WIDTH), tok(BRANCH_WIDTH), _const_spec((1, GLA_DV)),
                  heads(GQA_HEADS, GQA_HEAD_DIM), tok(BRANCH_WIDTH),
                  tok(N_BRANCH * D), _const_spec(wb.shape), _const_spec(wo.shape)],
        out_specs=tok(D),
        out_shape=jax.ShapeDtypeStruct((B, L, D), F32),
        compiler_params=_cparams(2),
        name="merge",
    )(x, g2, oa, ob, og, gla_g.reshape(1, GLA_DV), oc, od, gates, wb, wo)


GLA_PAIR = 2


def _gla_kernel(q_ref, k_ref, v_ref, laf_ref, lab_ref, sf0_ref, sb0_ref, o_ref, sf_ref, sb_ref, stf, stb, *, n_chunks):
    C = GLA_CHUNK
    ii = lax.broadcasted_iota(jnp.int32, (C, C), 0)
    jj = lax.broadcasted_iota(jnp.int32, (C, C), 1)
    low, up = ii >= jj, ii <= jj
    tri_low = jnp.where(low, 1.0, 0.0).astype(BF16)
    tri_up = jnp.where(up, 1.0, 0.0).astype(BF16)
    nt = (((1,), (1,)), ((), ()))
    stf[...] = sf0_ref[0]
    stb[...] = sb0_ref[0]

    def one_direction(n, tri, mask, la_ref, st, last_row, accumulate):
        rows = pl.ds(pl.multiple_of(n * C, C), C)
        q, k, v, la = q_ref[0, rows, :], k_ref[0, rows, :], v_ref[0, rows, :], la_ref[0, rows, :]
        hi = la.astype(BF16)
        lo = (la - hi.astype(F32)).astype(BF16)
        bc = jnp.dot(tri, hi, preferred_element_type=F32) + jnp.dot(tri, lo, preferred_element_type=F32)
        b_last = bc[last_row:last_row + 1, :]
        q_dec = (q * jnp.exp(bc)).astype(BF16)
        k_inv = (k * jnp.exp(-bc)).astype(BF16)
        k_end = (k * jnp.exp(b_last - bc)).astype(BF16)
        decay = jnp.exp(b_last)
        outs = []
        for h in range(GLA_PAIR):
            ks = slice(h * GLA_DK, (h + 1) * GLA_DK)
            vh = v[:, h * GLA_DV:(h + 1) * GLA_DV]
            a = lax.dot_general(q_dec[:, ks], k_inv[:, ks], nt, preferred_element_type=F32)
            a = jnp.where(mask, a, 0.0).astype(BF16)
            s_t = st[h]
            outs.append(jnp.dot(a, vh.astype(BF16), preferred_element_type=F32)
                        + lax.dot_general(q_dec[:, ks], s_t.astype(BF16), nt, preferred_element_type=F32))
            st[h] = s_t * decay[:, ks] + jnp.dot(vh.T.astype(BF16), k_end[:, ks], preferred_element_type=F32)
        o = jnp.concatenate(outs, axis=-1)
        if accumulate:
            o_ref[0, rows, :] = o_ref[0, rows, :] + o
        else:
            o_ref[0, rows, :] = o

    def sweep(accumulate):
        def body(n, carry):
            one_direction(n, tri_low, low, laf_ref, stf, C - 1, accumulate)
            one_direction(n_chunks - 1 - n, tri_up, up, lab_ref, stb, 0, accumulate)
            return carry
        return body

    half = n_chunks // 2
    lax.fori_loop(0, half, sweep(False), 0)
    lax.fori_loop(half, n_chunks, sweep(True), 0)
    sf_ref[0] = stf[...]
    sb_ref[0] = stb[...]


def _gla(q, k, v, la, sf0, sb0):
    B, L, _ = q.shape
    n_chunks = L // GLA_CHUNK
    assert L % GLA_CHUNK == 0 and n_chunks % 2 == 0
    wk, wv, npair = GLA_PAIR * GLA_DK, GLA_PAIR * GLA_DV, GLA_HEADS // GLA_PAIR
    seq = lambda w, off: pl.BlockSpec((1, L, w), lambda b, p: (b, 0, p + off))
    st_spec = pl.BlockSpec((1, GLA_PAIR, GLA_DV, GLA_DK), lambda b, p: (b, p, 0, 0))
    st_shape = jax.ShapeDtypeStruct((B, GLA_HEADS, GLA_DV, GLA_DK), F32)
    return pl.pallas_call(
        functools.partial(_gla_kernel, n_chunks=n_chunks),
        grid=(B, npair),
        in_specs=[seq(wk, 0), seq(wk, 0), seq(wv, 0), seq(wk, 0), seq(wk, npair), st_spec, st_spec],
        out_specs=(seq(wv, 0), st_spec, st_spec),
        out_shape=(jax.ShapeDtypeStruct((B, L, GLA_HEADS * GLA_DV), F32), st_shape, st_shape),
        scratch_shapes=[pltpu.VMEM((GLA_PAIR, GLA_DV, GLA_DK), F32), pltpu.VMEM((GLA_PAIR, GLA_DV, GLA_DK), F32)],
        compiler_params=_cparams(2),
        name="gla",
    )(q, k, v, la, la, sf0, sb0)


HY_LANE = 128


def _sconv_kernel(p1_ref, p2_ref, p3_ref, w1_ref, w2_ref, w3_ref, b1_ref, b2_ref, b3_ref, x1_ref, x2_ref, v_ref):
    L = p1_ref.shape[1]
    row = lax.broadcasted_iota(jnp.int32, (L, HY_LANE), 0)
    for p_ref, w_ref, b_ref, o_ref in ((p1_ref, w1_ref, b1_ref, x1_ref), (p2_ref, w2_ref, b2_ref, x2_ref),
                                       (p3_ref, w3_ref, b3_ref, v_ref)):
        x = p_ref[0]
        prev = jnp.where(row == 0, 0.0, pltpu.roll(x, 1, 0))
        nxt = jnp.where(row == L - 1, 0.0, pltpu.roll(x, L - 1, 0))
        o_ref[0] = prev * w_ref[0:1, :] + x * w_ref[1:2, :] + nxt * w_ref[2:3, :] + b_ref[...]


def _sconv(p, w, b):
    B, L, _ = p.shape
    nb = HY_WIDTH // HY_LANE
    pspec = lambda g: pl.BlockSpec((1, L, HY_LANE), lambda bb, j: (bb, 0, g * nb + j))
    wspec = lambda g: pl.BlockSpec((3, HY_LANE), lambda bb, j: (0, g * nb + j))
    bspec = lambda g: pl.BlockSpec((1, HY_LANE), lambda bb, j: (0, g * nb + j))
    ospec = pl.BlockSpec((1, L, HY_LANE), lambda bb, j: (bb, 0, j))
    oshape = jax.ShapeDtypeStruct((B, L, HY_WIDTH), F32)
    b2 = b.reshape(1, -1)
    return pl.pallas_call(
        _sconv_kernel,
        grid=(B, nb),
        in_specs=[pspec(0), pspec(1), pspec(2), wspec(0), wspec(1), wspec(2), bspec(0), bspec(1), bspec(2)],
        out_specs=(ospec, ospec, ospec),
        out_shape=(oshape, oshape, oshape),
        compiler_params=_cparams(2),
        name="hy_sconv",
    )(p, p, p, w, w, w, b2, b2, b2)


def _hy_filter_kernel(feat_ref, w1_ref, b1_ref, w2_ref, b2_ref, w3_ref, delta_ref, o_ref):
    hp = lax.Precision.HIGHEST
    feat = feat_ref[...]
    h = jnp.sin(HY_SIN_FREQ * (jnp.dot(feat, w1_ref[...], precision=hp, preferred_element_type=F32) + b1_ref[...]))
    h = jnp.sin(HY_SIN_FREQ * (jnp.dot(h, w2_ref[...], precision=hp, preferred_element_type=F32) + b2_ref[...]))
    h = jnp.dot(h, w3_ref[...], precision=hp, preferred_element_type=F32)
    win = jnp.exp(-feat[:, 0:1] * delta_ref[...])
    lag0 = (pl.program_id(0) == 0) & (lax.broadcasted_iota(jnp.int32, win.shape, 0) == 0)
    win_b = jnp.where(lag0, 0.0, win)
    for o in range(HY_ORDER):
        for d, wnd in enumerate((win, win_b)):
            lo = (o * 2 + d) * HY_WIDTH
            o_ref[0, :, lo:lo + HY_WIDTH] = h[:, lo:lo + HY_WIDTH] * wnd


def _hy_filters(L, w1, b1, w2, b2, w3):
    feat, deltas = _hy_feat_consts(L)
    emb = w1.shape[0]
    w1p = jnp.zeros((HY_LANE, w1.shape[1]), F32).at[:emb].set(w1)
    tl = _pick_tile(L, 512)
    nf = w3.shape[1]
    return pl.pallas_call(
        _hy_filter_kernel,
        grid=(L // tl,),
        in_specs=[pl.BlockSpec((tl, HY_LANE), lambda i: (i, 0)), _const_spec(w1p.shape), _const_spec((1, w1.shape[1])),
                  _const_spec(w2.shape), _const_spec((1, w2.shape[1])), _const_spec(w3.shape),
                  _const_spec((1, HY_WIDTH))],
        out_specs=pl.BlockSpec((1, tl, nf), lambda i: (0, i, 0)),
        out_shape=jax.ShapeDtypeStruct((1, L, nf), F32),
        compiler_params=_cparams(1),
        name="hy_filter",
    )(jnp.asarray(feat), w1p, b1.reshape(1, -1), w2, b2.reshape(1, -1), w3, jnp.asarray(deltas))


def _dft_left_kernel(m_ref, x_ref, o_ref):
    o_ref[0] = jnp.dot(m_ref[...], x_ref[0].astype(BF16), preferred_element_type=F32).astype(o_ref.dtype)


def _dft_left(mat, x):
    B, K, W = x.shape
    R = mat.shape[0]
    tw = _pick_tile(W, 4096)
    return pl.pallas_call(
        _dft_left_kernel,
        grid=(B, W // tw),
        in_specs=[_const_spec(mat.shape), pl.BlockSpec((1, K, tw), lambda b, j: (b, 0, j))],
        out_specs=pl.BlockSpec((1, R, tw), lambda b, j: (b, 0, j)),
        out_shape=jax.ShapeDtypeStruct((B, R, W), BF16),
        compiler_params=_cparams(2),
        name="hy_dft_in",
    )(mat, x)


def _dft_spectrum_kernel(a_ref, mf_ref, o_ref, *, kb, n2):
    for s in range(kb):
        a = jnp.concatenate([a_ref[0, 0, s], a_ref[0, 1, s]], axis=0)
        x = jnp.dot(mf_ref[s], a, preferred_element_type=F32)
        xr, xi = x[:n2], x[n2:]
        for o in range(HY_ORDER):
            f = slice((2 * o) * HY_WIDTH, (2 * o + 1) * HY_WIDTH)
            b = slice((2 * o + 1) * HY_WIDTH, (2 * o + 2) * HY_WIDTH)
            o_ref[o, 0, s] = xr[:, f] + xr[:, b]
            o_ref[o, 1, s] = xi[:, f] - xi[:, b]


def _dft_spectrum(a, mf):
    _, _, n1, n2, nf = a.shape
    kb = min(4, n1)
    return pl.pallas_call(
        functools.partial(_dft_spectrum_kernel, kb=kb, n2=n2),
        grid=(n1 // kb,),
        in_specs=[pl.BlockSpec((1, 2, kb, n2, nf), lambda i: (0, 0, i, 0, 0)),
                  pl.BlockSpec((kb, 2 * n2, 2 * n2), lambda i: (i, 0, 0))],
        out_specs=pl.BlockSpec((HY_ORDER, 2, kb, n2, HY_WIDTH), lambda i: (0, 0, i, 0, 0)),
        out_shape=jax.ShapeDtypeStruct((HY_ORDER, 2, n1, n2, HY_WIDTH), F32),
        compiler_params=_cparams(1),
        name="hy_spectrum",
    )(a, mf)


def _dft_mid_kernel(a_ref, h_ref, mf_ref, mi_ref, o_ref, *, kb, n2):
    for s in range(kb):
        a = jnp.concatenate([a_ref[0, 0, s], a_ref[0, 1, s]], axis=0)
        x = jnp.dot(mf_ref[s], a, preferred_element_type=F32)
        xr, xi = x[:n2], x[n2:]
        hr, hi = h_ref[0, s], h_ref[1, s]
        y = jnp.concatenate([xr * hr - xi * hi, xr * hi + xi * hr], axis=0).astype(BF16)
        b = jnp.dot(mi_ref[s], y, preferred_element_type=F32)
        o_ref[0, 0, s] = b[:n2].astype(BF16)
        o_ref[0, 1, s] = b[n2:].astype(BF16)


def _dft_mid(a, h, mf, mi):
    B, _, n1, n2, W = a.shape
    kb = min(8, n1)
    return pl.pallas_call(
        functools.partial(_dft_mid_kernel, kb=kb, n2=n2),
        grid=(n1 // kb, B),
        in_specs=[pl.BlockSpec((1, 2, kb, n2, W), lambda i, b: (b, 0, i, 0, 0)),
                  pl.BlockSpec((2, kb, n2, W), lambda i, b: (0, i, 0, 0)),
                  pl.BlockSpec((kb, 2 * n2, 2 * n2), lambda i, b: (i, 0, 0)),
                  pl.BlockSpec((kb, 2 * n2, 2 * n2), lambda i, b: (i, 0, 0))],
        out_specs=pl.BlockSpec((1, 2, kb, n2, W), lambda i, b: (b, 0, i, 0, 0)),
        out_shape=jax.ShapeDtypeStruct((B, 2, n1, n2, W), BF16),
        compiler_params=_cparams(2),
        name="hy_dft_mid",
    )(a, h, mf, mi)


def _dft_out_kernel(m_ref, b_ref, u_ref, g_ref, bias_ref, o_ref):
    y = jnp.dot(m_ref[...], b_ref[0], preferred_element_type=F32)
    o_ref[0] = g_ref[0] * (y + u_ref[0] * bias_ref[...])


def _dft_out(mat, bp, u, gate, bias_row):
    B, K, W = bp.shape
    R = mat.shape[0]
    tw = bias_row.shape[1]
    row = lambda r: pl.BlockSpec((1, r, tw), lambda b, j: (b, 0, j))
    return pl.pallas_call(
        _dft_out_kernel,
        grid=(B, W // tw),
        in_specs=[_const_spec(mat.shape), row(K), row(R), row(R), _const_spec((1, tw))],
        out_specs=row(R),
        out_shape=jax.ShapeDtypeStruct((B, R, W), F32),
        compiler_params=_cparams(2),
        name="hy_dft_out",
    )(mat, bp, u, gate, bias_row)


@functools.lru_cache(maxsize=None)
def _hy_feat_consts(L):
    t = np.linspace(0.0, 1.0, L)
    w = 2.0 * np.pi * np.arange(L) / L
    f = np.linspace(1e-4, HY_BANDS - 1, HY_BANDS)
    fw = w[:, None] * f[None, :]
    feat = np.zeros((L, HY_LANE), np.float32)
    feat[:, :2 * HY_BANDS + 1] = np.concatenate([t[:, None], np.cos(fw), -np.sin(fw)], axis=-1)
    deltas = np.abs(np.linspace(math.log(HY_DECAY_TARGET) / HY_LONG_DECAY_PCT,
                                math.log(HY_DECAY_TARGET) / HY_SHORT_DECAY_PCT, HY_WIDTH))
    return feat, deltas.reshape(1, HY_WIDTH).astype(np.float32)


@functools.lru_cache(maxsize=None)
def _dft_consts(L):
    N = 2 * L
    n2 = 64 if L >= 2048 else 32
    n1 = N // n2
    pad_input = n1 // 2 < 16
    kin = n1 if pad_input else n1 // 2
    k1 = np.arange(n1)
    f1 = np.exp(-2j * np.pi * np.outer(k1, np.arange(kin)) / n1)
    mat_in = np.concatenate([f1.real, f1.imag], axis=0)
    a = np.arange(n2)
    f2 = np.exp(-2j * np.pi * np.outer(a, a) / n2)
    tw = np.exp(-2j * np.pi * np.outer(k1, a) / N)
    mf = f2[None, :, :] * tw[:, None, :]
    mi = np.conj(np.transpose(mf, (0, 2, 1)))
    blockform = lambda m: np.concatenate([np.concatenate([m.real, -m.imag], axis=2),
                                          np.concatenate([m.imag, m.real], axis=2)], axis=1)
    g = np.exp(2j * np.pi * np.outer(np.arange(n1 // 2), k1) / n1) / N
    mat_out = np.concatenate([g.real, -g.imag], axis=1)
    f32 = lambda m: np.ascontiguousarray(m, dtype=np.float32)
    return dict(n1=n1, n2=n2, pad_input=pad_input, mat_in=f32(mat_in), mf=f32(blockform(mf)), mi=f32(blockform(mi)),
                mat_out=f32(mat_out))


def _hyena(dp, sconv_w, sconv_b, w1, b1, w2, b2, w3, fbias):
    B, L, _ = dp.shape
    W = HY_WIDTH
    cst = _dft_consts(L)
    n1, n2 = cst["n1"], cst["n2"]
    mat_in, mat_out = jnp.asarray(cst["mat_in"], BF16), jnp.asarray(cst["mat_out"], BF16)
    mf, mi = jnp.asarray(cst["mf"], BF16), jnp.asarray(cst["mi"], BF16)

    def stage_in(u):
        b, _, c = u.shape
        if cst["pad_input"]:
            u = jnp.concatenate([u, jnp.zeros_like(u)], axis=1)
        a = _dft_left(mat_in, u.reshape(b, -1, n2 * c))
        return a.reshape(b, 2, n1, n2, c)

    x1, x2, v = _sconv(dp, sconv_w, sconv_b)
    spec = _dft_spectrum(stage_in(_hy_filters(L, w1, b1, w2, b2, w3)), mf)
    tw = _pick_tile(n2 * W, 4096)
    z = v
    for o, gate in enumerate((x1, x2)):
        bp = _dft_mid(stage_in(z), spec[o], mf, mi).reshape(B, 2 * n1, n2 * W)
        bias_row = jnp.tile(fbias[o].reshape(1, W), (1, tw // W))
        z = _dft_out(mat_out, bp, z.reshape(B, n1 // 2, n2 * W), gate.reshape(B, n1 // 2, n2 * W), bias_row)
        z = z.reshape(B, L, W)
    return z


def _swap_perm(width):
    q = width // 4
    return np.concatenate([np.arange(q, 2 * q), np.arange(0, q), np.arange(3 * q, 4 * q), np.arange(2 * q, 3 * q)])


def _rope_tables(L, width, latent):
    if not latent:
        return jnp.ones((L, width), F32), jnp.zeros((L, width), F32)
    q = width // 4
    t = jnp.arange(L, dtype=jnp.int32)
    row = (t // GRID_W).astype(F32)
    col = (t % GRID_W).astype(F32)
    freqs = ROPE_THETA ** (-jnp.arange(q, dtype=F32) / q)
    ar, ac = row[:, None] * freqs[None, :], col[:, None] * freqs[None, :]
    cos = jnp.concatenate([jnp.cos(ar), jnp.cos(ar), jnp.cos(ac), jnp.cos(ac)], axis=-1)
    sin = jnp.concatenate([-jnp.sin(ar), jnp.sin(ar), -jnp.sin(ac), jnp.sin(ac)], axis=-1)
    return cos, sin


def _tables(L, latent):
    ca, sa = _rope_tables(L, MLA_ROPE, latent)
    pad = lambda t, fill: jnp.concatenate(
        [jnp.full((L, MLA_NOPE), fill, F32), t, jnp.zeros((L, MLA_DPAD - MLA_NOPE - MLA_ROPE), F32)], axis=-1)
    cc, sc = _rope_tables(L, GQA_HEAD_DIM, latent)
    return pad(ca, 1.0), pad(sa, 0.0), cc, sc


def _prep_layer_weights(w_in, w_uq, w_ukv, gmq, gmkv, gla_w_gate, gla_b_gate, gcq, gck):
    D = w_in.shape[0]
    col = lambda i: w_in[:, IN_OFFS[i]:IN_OFFS[i + 1]]
    a_cq, a_ckv, a_kr, b_q, b_k, b_v, b_lr, b_og, c_q, c_k, c_v, d_p, gates = [col(i) for i in range(13)]
    p32, p64 = _swap_perm(MLA_ROPE), _swap_perm(GQA_HEAD_DIM)
    z = lambda n: jnp.zeros((D, n), F32)
    pad_kr = lambda t: jnp.concatenate([z(MLA_NOPE), t, z(MLA_DPAD - MLA_NOPE - MLA_ROPE)], axis=1)
    swap_heads = lambda t, n: t.reshape(D, n, GQA_HEAD_DIM)[:, :, p64].reshape(D, n * GQA_HEAD_DIM)
    w_att = jnp.concatenate([a_cq, a_ckv, pad_kr(a_kr), pad_kr(a_kr[:, p32]),
                             c_q, swap_heads(c_q, GQA_HEADS), c_k, swap_heads(c_k, GQA_KV_HEADS), c_v], axis=1)
    w_mix = jnp.concatenate([b_q, b_k, b_v, b_og, b_lr, z(128 - 2 * GLA_GATE_RANK), d_p, gates], axis=1)

    uq = w_uq.reshape(MLA_Q_LORA, MLA_HEADS, MLA_NOPE + MLA_ROPE)
    zq = jnp.zeros((MLA_Q_LORA, MLA_HEADS, MLA_DPAD - MLA_NOPE - MLA_ROPE), F32)
    uq_pad = jnp.concatenate([uq, zq], axis=-1).reshape(MLA_Q_LORA, MLA_HEADS * MLA_DPAD)
    uq_sw = jnp.concatenate([uq[..., :MLA_NOPE], uq[..., MLA_NOPE:][..., p32], zq], axis=-1)
    wuq2 = jnp.concatenate([uq_pad, uq_sw.reshape(MLA_Q_LORA, MLA_HEADS * MLA_DPAD)], axis=1)
    ukv = w_ukv.reshape(MLA_KV_LORA, MLA_HEADS, MLA_NOPE + MLA_V)
    zk = jnp.zeros((MLA_KV_LORA, MLA_HEADS, MLA_DPAD - MLA_NOPE), F32)
    uk = jnp.concatenate([ukv[..., :MLA_NOPE], zk], axis=-1).reshape(MLA_KV_LORA, MLA_HEADS * MLA_DPAD)
    uv = ukv[..., MLA_NOPE:].reshape(MLA_KV_LORA, MLA_HEADS * MLA_V)
    wukv2 = jnp.concatenate([uk, uv], axis=1)

    hk = GLA_HEADS * GLA_DK
    wgate = jnp.zeros((128, 2 * hk), F32)
    wgate = wgate.at[:GLA_GATE_RANK, :hk].set(gla_w_gate[0]).at[GLA_GATE_RANK:2 * GLA_GATE_RANK, hk:].set(gla_w_gate[1])
    return dict(w_att=w_att.astype(BF16), w_mix=w_mix.astype(BF16), wuq2=wuq2.astype(BF16), wukv2=wukv2.astype(BF16),
                gmq=gmq.reshape(1, -1), gmkv=gmkv.reshape(1, -1),
                gcq=jnp.stack([gcq, gcq[p64]]), gck=jnp.stack([gck, gck[p64]]),
                wgate=wgate.astype(BF16), bgate=gla_b_gate.reshape(1, 2 * hk))


def kernel(x, c, ctx, c_ctx, ada_w, ada_b, norm_g, ffn_w_gate, ffn_w_up, ffn_w_down, w_in, mla_q_norm_g, mla_w_uq, mla_kv_norm_g, mla_w_ukv, gla_w_gate, gla_b_gate, gla_norm_g, gqa_q_norm_g, gqa_k_norm_g, hy_sconv_w, hy_sconv_b, hy_filt_w1, hy_filt_b1, hy_filt_w2, hy_filt_b2, hy_filt_w3, hy_filt_bias, w_branch, w_out, final_g):
    B, L, D = x.shape
    Lc = ctx.shape[1]
    depth = ada_w.shape[0]
    tabs_lat = _tables(L, True)
    tabs_ctx = _tables(Lc, False)
    cvec = jnp.concatenate([c, c_ctx[None]], axis=0)
    xc = ctx
    for l in range(depth):
        need_ctx = l < depth - 1
        mod = _ada(cvec, ada_w[l], ada_b[l]).reshape(B + 1, N_ADA, D)
        mod_lat = mod[:B]
        mod_ctx = jnp.broadcast_to(mod[B:], (B, N_ADA, D))
        wg, wu, wd = ffn_w_gate[l].astype(BF16), ffn_w_up[l].astype(BF16), ffn_w_down[l].astype(BF16)
        wp = _prep_layer_weights(w_in[l], mla_w_uq[l], mla_w_ukv[l], mla_q_norm_g[l], mla_kv_norm_g[l],
                                 gla_w_gate[l], gla_b_gate[l], gqa_q_norm_g[l], gqa_k_norm_g[l])
        wb, wo = w_branch[l].astype(BF16), w_out[l].astype(BF16)

        x = _ffn(x, mod_lat[:, 0:3], norm_g[l, 0], wg[0], wu[0], wd[0])
        xc = _ffn(xc, mod_ctx[:, 0:3], norm_g[l, 0], wg[0], wu[0], wd[0])

        qa, ka, va, qc, kc, vc = _attn_prep(x, mod_lat[:, 3:6], norm_g[l, 1], wp, tabs_lat)
        qa_c, ka_c, va_c, qc_c, kc_c, vc_c = _attn_prep(xc, mod_ctx[:, 3:6], norm_g[l, 1], wp, tabs_ctx)
        gq, gk, gv, la, og, dp, gates = _mix_prep(x, mod_lat[:, 3:6], norm_g[l, 1], wp)
        gq_c, gk_c, gv_c, la_c, og_c, dp_c, gates_c = _mix_prep(xc, mod_ctx[:, 3:6], norm_g[l, 1], wp)

        cat = lambda a, b: jnp.concatenate([a, b], axis=2)
        o_a = _attend(qa, cat(ka, ka_c), cat(va, va_c))
        qc_g = qc.reshape(B, GQA_KV_HEADS, (GQA_HEADS // GQA_KV_HEADS) * L, GQA_HEAD_DIM)
        o_c = _attend(qc_g, cat(kc, kc_c), cat(vc, vc_c)).reshape(B, GQA_HEADS, L, GQA_HEAD_DIM)

        s0 = jnp.zeros((B, GLA_HEADS, GLA_DV, GLA_DK), F32)
        ob_c, s_f, s_b = _gla(gq_c, gk_c, gv_c, la_c, s0, s0)
        o_b, _, _ = _gla(gq, gk, gv, la, s_f, s_b)

        hy = (hy_sconv_w[l], hy_sconv_b[l], hy_filt_w1[l], hy_filt_b1[l], hy_filt_w2[l], hy_filt_b2[l],
              hy_filt_w3[l], hy_filt_bias[l])
        o_d = _hyena(dp, *hy)

        x = _merge(x, mod_lat[:, 5:6], o_a, o_b, og, gla_norm_g[l], o_c, o_d, gates, wb, wo)
        last = l == depth - 1
        x = _ffn(x, mod_lat[:, 6:9], norm_g[l, 2], wg[1], wu[1], wd[1], final_g=final_g if last else None)
        if need_ctx:
            o_ac = _attend(qa_c, ka_c, va_c)
            qcc_g = qc_c.reshape(B, GQA_KV_HEADS, (GQA_HEADS // GQA_KV_HEADS) * Lc, GQA_HEAD_DIM)
            o_cc = _attend(qcc_g, kc_c, vc_c).reshape(B, GQA_HEADS, Lc, GQA_HEAD_DIM)
            o_dc = _hyena(dp_c, *hy)
            xc = _merge(xc, mod_ctx[:, 5:6], o_ac, ob_c, og_c, gla_norm_g[l], o_cc, o_dc, gates_c, wb, wo)
            xc = _ffn(xc, mod_ctx[:, 6:9], norm_g[l, 2], wg[1], wu[1], wd[1])
    return x
```

```python
import functools
import math

import jax
import jax.numpy as jnp
import numpy as np
from jax import lax
from jax.experimental import pallas as pl
from jax.experimental.pallas import tpu as pltpu

F32 = jnp.float32
BF16 = jnp.bfloat16

D_MODEL = 1024
GRID_W = 64
N_ADA = 9
FFN_DIM = 2816
RMS_EPS = 1e-6
ROPE_THETA = 10000.0
MLA_HEADS = 8
MLA_NOPE = 64
MLA_ROPE = 32
MLA_V = 64
MLA_Q_LORA = 256
MLA_KV_LORA = 128
MLA_SCALE = (MLA_NOPE + MLA_ROPE) ** -0.5
MLA_DPAD = 128
GLA_HEADS = 4
GLA_DK = 64
GLA_DV = 128
GLA_GATE_RANK = 16
GLA_GATE_TEMP = 16.0
GLA_CHUNK = 64
GQA_HEADS = 8
GQA_KV_HEADS = 2
GQA_HEAD_DIM = 64
GQA_SCALE = GQA_HEAD_DIM ** -0.5
HY_WIDTH = 512
HY_ORDER = 2
HY_BANDS = 16
HY_SIN_FREQ = 1.0
HY_DECAY_TARGET = 1e-2
HY_SHORT_DECAY_PCT = 0.3
HY_LONG_DECAY_PCT = 1.5
N_BRANCH = 4
BRANCH_WIDTH = 512
LOG2E = math.log2(math.e)

IN_SIZES = (MLA_Q_LORA, MLA_KV_LORA, MLA_ROPE,
            GLA_HEADS * GLA_DK, GLA_HEADS * GLA_DK, GLA_HEADS * GLA_DV, 2 * GLA_GATE_RANK, GLA_HEADS * GLA_DV,
            GQA_HEADS * GQA_HEAD_DIM, GQA_KV_HEADS * GQA_HEAD_DIM, GQA_KV_HEADS * GQA_HEAD_DIM,
            (HY_ORDER + 1) * HY_WIDTH,
            N_BRANCH * D_MODEL)
IN_OFFS = tuple(int(v) for v in np.cumsum((0,) + IN_SIZES))

VMEM_LIMIT_BYTES = 56 * 1024 * 1024


def _cparams(n_axes):
    return pltpu.CompilerParams(dimension_semantics=("arbitrary",) * n_axes,
                                vmem_limit_bytes=VMEM_LIMIT_BYTES)


def _const_spec(shape):
    nd = len(shape)
    return pl.BlockSpec(shape, lambda *_: (0,) * nd, pipeline_mode=pl.Buffered(1))


def _pick_tile(n, pref):
    t = min(n, pref)
    while n % t:
        t //= 2
    return t


def _rms(x):
    return x * lax.rsqrt(jnp.mean(x * x, axis=-1, keepdims=True) + RMS_EPS)


def _norm_mod(x, g, shift, scale):
    return (_rms(x) * g) * (1.0 + scale) + shift


def _ada_kernel(s_ref, w_ref, b_ref, o_ref):
    s = s_ref[...]
    s = s * jax.nn.sigmoid(s)
    o_ref[...] = jnp.dot(s.astype(BF16), w_ref[...].astype(BF16), preferred_element_type=F32) + b_ref[...]


def _ada(cvec, w, b):
    rows, d = cvec.shape
    n = w.shape[1]
    tn = _pick_tile(n, 1152)
    return pl.pallas_call(
        _ada_kernel,
        grid=(n // tn,),
        in_specs=[pl.BlockSpec((rows, d), lambda j: (0, 0)),
                  pl.BlockSpec((d, tn), lambda j: (0, j)),
                  pl.BlockSpec((1, tn), lambda j: (0, j))],
        out_specs=pl.BlockSpec((rows, tn), lambda j: (0, j)),
        out_shape=jax.ShapeDtypeStruct((rows, n), F32),
        compiler_params=_cparams(1),
        name="ada_mod",
    )(cvec, w, b.reshape(1, n))


def _ffn_kernel(x_ref, mod_ref, g_ref, wg_ref, wu_ref, wd_ref, fg_ref, o_ref, *, final_norm):
    x = x_ref[0]
    mod = mod_ref[0]
    h = _norm_mod(x, g_ref[...], mod[0:1], mod[1:2]).astype(BF16)
    a = jnp.dot(h, wg_ref[...], preferred_element_type=F32)
    b = jnp.dot(h, wu_ref[...], preferred_element_type=F32)
    act = (a * jax.nn.sigmoid(a) * b).astype(BF16)
    y = jnp.dot(act, wd_ref[...], preferred_element_type=F32)
    out = x + (0.5 * mod[2:3]) * y
    if final_norm:
        out = _rms(out) * fg_ref[...]
    o_ref[0] = out


def _ffn(x, mod3, g, wg, wu, wd, final_g=None):
    B, L, D = x.shape
    F = wg.shape[1]
    tm = _pick_tile(L, 512)
    final_norm = final_g is not None
    fg = (final_g if final_norm else jnp.ones((D,), F32)).reshape(1, D)
    return pl.pallas_call(
        functools.partial(_ffn_kernel, final_norm=final_norm),
        grid=(B, L // tm),
        in_specs=[pl.BlockSpec((1, tm, D), lambda b, i: (b, i, 0)),
                  pl.BlockSpec((1, 3, D), lambda b, i: (b, 0, 0)),
                  _const_spec((1, D)), _const_spec((D, F)), _const_spec((D, F)), _const_spec((F, D)),
                  _const_spec((1, D))],
        out_specs=pl.BlockSpec((1, tm, D), lambda b, i: (b, i, 0)),
        out_shape=jax.ShapeDtypeStruct((B, L, D), F32),
        compiler_params=_cparams(2),
        name="ffn",
    )(x, mod3, g.reshape(1, D), wg, wu, wd, fg)


QA_SCALE = MLA_SCALE * LOG2E
QC_SCALE = GQA_SCALE * LOG2E
_A_CQ, _A_CKV, _A_KR, _A_KRS, _C_Q, _C_QS, _C_K, _C_KS, _C_V, _ATT_COLS = (
    0, 256, 384, 512, 640, 1152, 1664, 1792, 1920, 2048)


def _attn_prep_kernel(x_ref, mod_ref, g_ref, w_ref, wuq_ref, wukv_ref, gmq_ref, gmkv_ref, gcq_ref, gck_ref,
                      cosa_ref, sina_ref, cosc_ref, sinc_ref,
                      qa_ref, ka_ref, va_ref, qc_ref, kc_ref, vc_ref):
    x = x_ref[0]
    mod = mod_ref[0]
    h = _norm_mod(x, g_ref[...], mod[0:1], mod[1:2]).astype(BF16)
    p = jnp.dot(h, w_ref[...], preferred_element_type=F32)
    cosa, sina = cosa_ref[...], sina_ref[...]
    cosc, sinc = cosc_ref[...], sinc_ref[...]

    kr = p[:, _A_KR:_A_KR + 128] * cosa + p[:, _A_KRS:_A_KRS + 128] * sina
    nq = (_rms(p[:, _A_CQ:_A_CQ + MLA_Q_LORA]) * gmq_ref[...]).astype(BF16)
    q2 = jnp.dot(nq, wuq_ref[...], preferred_element_type=F32)
    nkv = (_rms(p[:, _A_CKV:_A_CKV + MLA_KV_LORA]) * gmkv_ref[...]).astype(BF16)
    kv = jnp.dot(nkv, wukv_ref[...], preferred_element_type=F32)
    hq = MLA_HEADS * MLA_DPAD
    for hd in range(MLA_HEADS):
        lo = hd * MLA_DPAD
        qh = q2[:, lo:lo + MLA_DPAD] * cosa + q2[:, hq + lo:hq + lo + MLA_DPAD] * sina
        qa_ref[0, hd] = (qh * QA_SCALE).astype(BF16)
        ka_ref[0, hd] = (kv[:, lo:lo + MLA_DPAD] + kr).astype(BF16)
        va_ref[0, hd] = kv[:, hq + hd * MLA_V:hq + (hd + 1) * MLA_V].astype(BF16)

    def normed_rot(t, ts, gains):
        inv = lax.rsqrt(jnp.mean(t * t, axis=-1, keepdims=True) + RMS_EPS)
        return ((t * inv) * gains[0:1]) * cosc + ((ts * inv) * gains[1:2]) * sinc

    gq, gk = gcq_ref[...], gck_ref[...]
    dh = GQA_HEAD_DIM
    for hd in range(GQA_HEADS):
        qh = normed_rot(p[:, _C_Q + hd * dh:_C_Q + (hd + 1) * dh], p[:, _C_QS + hd * dh:_C_QS + (hd + 1) * dh], gq)
        qc_ref[0, hd] = (qh * QC_SCALE).astype(BF16)
    for hd in range(GQA_KV_HEADS):
        kh = normed_rot(p[:, _C_K + hd * dh:_C_K + (hd + 1) * dh], p[:, _C_KS + hd * dh:_C_KS + (hd + 1) * dh], gk)
        kc_ref[0, hd] = kh.astype(BF16)
        vc_ref[0, hd] = p[:, _C_V + hd * dh:_C_V + (hd + 1) * dh].astype(BF16)


def _attn_prep(x, mod3, g, wp, tabs):
    B, L, D = x.shape
    tm = _pick_tile(L, 512)
    cosa, sina, cosc, sinc = tabs
    tok = lambda w: pl.BlockSpec((1, tm, w), lambda b, i: (b, i, 0))
    heads = lambda n, w: pl.BlockSpec((1, n, tm, w), lambda b, i: (b, 0, i, 0))
    tab = lambda w: pl.BlockSpec((tm, w), lambda b, i: (i, 0))
    out_shape = (jax.ShapeDtypeStruct((B, MLA_HEADS, L, MLA_DPAD), BF16),
                 jax.ShapeDtypeStruct((B, MLA_HEADS, L, MLA_DPAD), BF16),
                 jax.ShapeDtypeStruct((B, MLA_HEADS, L, MLA_V), BF16),
                 jax.ShapeDtypeStruct((B, GQA_HEADS, L, GQA_HEAD_DIM), BF16),
                 jax.ShapeDtypeStruct((B, GQA_KV_HEADS, L, GQA_HEAD_DIM), BF16),
                 jax.ShapeDtypeStruct((B, GQA_KV_HEADS, L, GQA_HEAD_DIM), BF16))
    return pl.pallas_call(
        _attn_prep_kernel,
        grid=(B, L // tm),
        in_specs=[tok(D), pl.BlockSpec((1, 3, D), lambda b, i: (b, 0, 0)), _const_spec((1, D)),
                  _const_spec(wp["w_att"].shape), _const_spec(wp["wuq2"].shape), _const_spec(wp["wukv2"].shape),
                  _const_spec((1, MLA_Q_LORA)), _const_spec((1, MLA_KV_LORA)),
                  _const_spec((2, GQA_HEAD_DIM)), _const_spec((2, GQA_HEAD_DIM)),
                  tab(MLA_DPAD), tab(MLA_DPAD), tab(GQA_HEAD_DIM), tab(GQA_HEAD_DIM)],
        out_specs=(heads(MLA_HEADS, MLA_DPAD), heads(MLA_HEADS, MLA_DPAD), heads(MLA_HEADS, MLA_V),
                   heads(GQA_HEADS, GQA_HEAD_DIM), heads(GQA_KV_HEADS, GQA_HEAD_DIM),
                   heads(GQA_KV_HEADS, GQA_HEAD_DIM)),
        out_shape=out_shape,
        compiler_params=_cparams(2),
        name="attn_prep",
    )(x, mod3, g.reshape(1, D), wp["w_att"], wp["wuq2"], wp["wukv2"], wp["gmq"], wp["gmkv"], wp["gcq"], wp["gck"],
      cosa, sina, cosc, sinc)


_B_Q, _B_K, _B_V, _B_OG, _B_LR, _D_P, _GATES, _MIX_COLS = (0, 256, 512, 1024, 1536, 1664, 3200, 7296)


def _mix_prep_kernel(x_ref, mod_ref, g_ref, w_ref, wgate_ref, bgate_ref,
                     gq_ref, gk_ref, gv_ref, la_ref, og_ref, dp_ref, gates_ref):
    x = x_ref[0]
    mod = mod_ref[0]
    h = _norm_mod(x, g_ref[...], mod[0:1], mod[1:2]).astype(BF16)
    pg = jnp.dot(h, w_ref[:, _B_Q:_D_P], preferred_element_type=F32)
    gq_ref[0] = pg[:, _B_Q:_B_K] * (GLA_DK ** -0.5)
    gk_ref[0] = pg[:, _B_K:_B_V]
    gv_ref[0] = pg[:, _B_V:_B_OG]
    og_ref[0] = pg[:, _B_OG:_B_LR]
    pre = jnp.dot(pg[:, _B_LR:_D_P].astype(BF16), wgate_ref[...], preferred_element_type=F32) + bgate_ref[...]
    la = (jnp.minimum(pre, 0.0) - jnp.log1p(jnp.exp(-jnp.abs(pre)))) * (1.0 / GLA_GATE_TEMP)
    C = GLA_CHUNK
    ii = lax.broadcasted_iota(jnp.int32, (C, C), 0)
    jj = lax.broadcasted_iota(jnp.int32, (C, C), 1)
    tri_low = jnp.where(ii >= jj, 1.0, 0.0).astype(BF16)
    tri_up = jnp.where(ii <= jj, 1.0, 0.0).astype(BF16)
    hi = la.astype(BF16)
    lo = (la - hi.astype(F32)).astype(BF16)
    hk = GLA_HEADS * GLA_DK
    for c in range(la.shape[0] // C):
        rs = slice(c * C, (c + 1) * C)
        la_ref[0, rs, :hk] = (jnp.dot(tri_low, hi[rs, :hk], preferred_element_type=F32)
                              + jnp.dot(tri_low, lo[rs, :hk], preferred_element_type=F32))
        la_ref[0, rs, hk:] = (jnp.dot(tri_up, hi[rs, hk:], preferred_element_type=F32)
                              + jnp.dot(tri_up, lo[rs, hk:], preferred_element_type=F32))
    dp_ref[0] = jnp.dot(h, w_ref[:, _D_P:_GATES], preferred_element_type=F32)
    gates_ref[0] = jax.nn.sigmoid(jnp.dot(h, w_ref[:, _GATES:_MIX_COLS], preferred_element_type=F32)).astype(BF16)


def _mix_prep(x, mod3, g, wp):
    B, L, D = x.shape
    tm = _pick_tile(L, 256)
    tok = lambda w: pl.BlockSpec((1, tm, w), lambda b, i: (b, i, 0))
    widths = (256, 256, 512, 512, 512, 1536, 4096)
    dtypes = (F32, F32, F32, F32, F32, F32, BF16)
    return pl.pallas_call(
        _mix_prep_kernel,
        grid=(B, L // tm),
        in_specs=[tok(D), pl.BlockSpec((1, 3, D), lambda b, i: (b, 0, 0)), _const_spec((1, D)),
                  _const_spec(wp["w_mix"].shape), _const_spec(wp["wgate"].shape), _const_spec((1, 512))],
        out_specs=tuple(tok(w) for w in widths),
        out_shape=tuple(jax.ShapeDtypeStruct((B, L, w), dt) for w, dt in zip(widths, dtypes)),
        compiler_params=_cparams(2),
        name="mix_prep",
    )(x, mod3, g.reshape(1, D), wp["w_mix"], wp["wgate"], wp["bgate"])


ATT_LANE = 128
ATT_TK = 256


def _lane_fold(x, op):
    acc = x[:, :ATT_LANE]
    for j in range(1, x.shape[1] // ATT_LANE):
        acc = op(acc, x[:, j * ATT_LANE:(j + 1) * ATT_LANE])
    return acc


def _attn_kernel(q_ref, k_ref, v_ref, o_ref, *, tk):
    q = q_ref[0, 0]
    nck = k_ref.shape[2] // tk
    nt = (((1,), (1,)), ((), ()))
    m = ls = acc = None
    for c in range(nck):
        s = lax.dot_general(q, k_ref[0, 0, c * tk:(c + 1) * tk, :], nt, preferred_element_type=F32)
        cm = jnp.max(s, axis=-1, keepdims=True)
        m_new = cm if c == 0 else jnp.maximum(m, cm)
        p = jnp.exp2(s - m_new)
        cl = _lane_fold(p, jnp.add)
        pv = jnp.dot(p.astype(BF16), v_ref[0, 0, c * tk:(c + 1) * tk, :], preferred_element_type=F32)
        if c == 0:
            ls, acc = cl, pv
        else:
            alpha = jnp.exp2(m - m_new)
            ls = alpha * ls + cl
            acc = alpha * acc + pv
        m = m_new
    l = jnp.sum(ls, axis=-1, keepdims=True)
    o_ref[0, 0] = (acc / l).astype(o_ref.dtype)


def _attend(q, k, v, group=1):
    B, H, Lq, d = q.shape
    Lk, dv = v.shape[2], v.shape[3]
    tq = _pick_tile(Lq, 512)
    tk = _pick_tile(Lk, ATT_TK)
    return pl.pallas_call(
        functools.partial(_attn_kernel, tk=tk),
        grid=(B, H, Lq // tq),
        in_specs=[pl.BlockSpec((1, 1, tq, d), lambda b, h, i: (b, h, i, 0)),
                  pl.BlockSpec((1, 1, Lk, d), lambda b, h, i: (b, h // group, 0, 0)),
                  pl.BlockSpec((1, 1, Lk, dv), lambda b, h, i: (b, h // group, 0, 0))],
        out_specs=pl.BlockSpec((1, 1, tq, dv), lambda b, h, i: (b, h, i, 0)),
        out_shape=jax.ShapeDtypeStruct((B, H, Lq, dv), BF16),
        compiler_params=_cparams(3),
        name="attend",
    )(q, k, v)


def _merge_kernel(x_ref, mod_ref, oa_ref, ob_ref, og_ref, gn_ref, oc_ref, od_ref, gates_ref, wb_ref, wo_ref, o_ref):
    oa = jnp.concatenate([oa_ref[0, hd] for hd in range(MLA_HEADS)], axis=-1)
    oc = jnp.concatenate([oc_ref[0, hd] for hd in range(GQA_HEADS)], axis=-1)
    ob = ob_ref[0]
    ob = jnp.concatenate([_rms(ob[:, hd * GLA_DV:(hd + 1) * GLA_DV]) * gn_ref[...] for hd in range(GLA_HEADS)], axis=-1)
    og = og_ref[0]
    ob = ob * (og * jax.nn.sigmoid(og))
    branches = (oa, ob.astype(BF16), oc, od_ref[0].astype(BF16))
    y = None
    for i, o in enumerate(branches):
        gate = gates_ref[0, :, i * D_MODEL:(i + 1) * D_MODEL].astype(F32)
        t = gate * jnp.dot(o, wb_ref[i], preferred_element_type=F32)
        y = t if y is None else y + t
    z = jnp.dot(y.astype(BF16), wo_ref[...], preferred_element_type=F32)
    o_ref[0] = x_ref[0] + mod_ref[0] * z


def _merge(x, g2, oa, ob, og, gla_g, oc, od, gates, wb, wo):
    B, L, D = x.shape
    tm = _pick_tile(L, 512)
    tok = lambda w: pl.BlockSpec((1, tm, w), lambda b, i: (b, i, 0))
    heads = lambda n, w: pl.BlockSpec((1, n, tm, w), lambda b, i: (b, 0, i, 0))
    return pl.pallas_call(
        _merge_kernel,
        grid=(B, L // tm),
        in_specs=[tok(D), pl.BlockSpec((1, 1, D), lambda b, i: (b, 0, 0)),
                  heads(MLA_HEADS, MLA_V), tok(BRANCH_WIDTH), tok(BRANCH_WIDTH), _const_spec((1, GLA_DV)),
                  heads(GQA_HEADS, GQA_HEAD_DIM), tok(BRANCH_WIDTH),
                  tok(N_BRANCH * D), _const_spec(wb.shape), _const_spec(wo.shape)],
        out_specs=tok(D),
        out_shape=jax.ShapeDtypeStruct((B, L, D), F32),
        compiler_params=_cparams(2),
        name="merge",
    )(x, g2, oa, ob, og, gla_g.reshape(1, GLA_DV), oc, od, gates, wb, wo)


GLA_PAIR = 2


def _gla_kernel(q_ref, k_ref, v_ref, laf_ref, lab_ref, sf0_ref, sb0_ref, o_ref, sf_ref, sb_ref, stf, stb, *, n_chunks):
    C = GLA_CHUNK
    ii = lax.broadcasted_iota(jnp.int32, (C, C), 0)
    jj = lax.broadcasted_iota(jnp.int32, (C, C), 1)
    low, up = ii >= jj, ii <= jj
    nt = (((1,), (1,)), ((), ()))
    stf[...] = sf0_ref[0]
    stb[...] = sb0_ref[0]

    def one_direction(n, mask, bc_ref, st, last_row, accumulate):
        rows = pl.ds(pl.multiple_of(n * C, C), C)
        q, k, v, bc = q_ref[0, rows, :], k_ref[0, rows, :], v_ref[0, rows, :], bc_ref[0, rows, :]
        b_last = bc[last_row:last_row + 1, :]
        q_dec = (q * jnp.exp(bc)).astype(BF16)
        k_inv = (k * jnp.exp(-bc)).astype(BF16)
        k_end = (k * jnp.exp(b_last - bc)).astype(BF16)
        decay = jnp.exp(b_last)
        outs = []
        for h in range(GLA_PAIR):
            ks = slice(h * GLA_DK, (h + 1) * GLA_DK)
            vh = v[:, h * GLA_DV:(h + 1) * GLA_DV]
            a = lax.dot_general(q_dec[:, ks], k_inv[:, ks], nt, preferred_element_type=F32)
            a = jnp.where(mask, a, 0.0).astype(BF16)
            s_t = st[h]
            outs.append(jnp.dot(a, vh.astype(BF16), preferred_element_type=F32)
                        + lax.dot_general(q_dec[:, ks], s_t.astype(BF16), nt, preferred_element_type=F32))
            st[h] = s_t * decay[:, ks] + jnp.dot(vh.T.astype(BF16), k_end[:, ks], preferred_element_type=F32)
        o = jnp.concatenate(outs, axis=-1)
        if accumulate:
            o_ref[0, rows, :] = o_ref[0, rows, :] + o
        else:
            o_ref[0, rows, :] = o

    def sweep(accumulate):
        def body(n, carry):
            one_direction(n, low, laf_ref, stf, C - 1, accumulate)
            one_direction(n_chunks - 1 - n, up, lab_ref, stb, 0, accumulate)
            return carry
        return body

    half = n_chunks // 2
    lax.fori_loop(0, half, sweep(False), 0, unroll=2)
    lax.fori_loop(half, n_chunks, sweep(True), 0, unroll=2)
    sf_ref[0] = stf[...]
    sb_ref[0] = stb[...]


def _gla(q, k, v, la, sf0, sb0):
    B, L, _ = q.shape
    n_chunks = L // GLA_CHUNK
    assert L % GLA_CHUNK == 0 and n_chunks % 2 == 0
    wk, wv, npair = GLA_PAIR * GLA_DK, GLA_PAIR * GLA_DV, GLA_HEADS // GLA_PAIR
    seq = lambda w, off: pl.BlockSpec((1, L, w), lambda b, p: (b, 0, p + off))
    st_spec = pl.BlockSpec((1, GLA_PAIR, GLA_DV, GLA_DK), lambda b, p: (b, p, 0, 0))
    st_shape = jax.ShapeDtypeStruct((B, GLA_HEADS, GLA_DV, GLA_DK), F32)
    return pl.pallas_call(
        functools.partial(_gla_kernel, n_chunks=n_chunks),
        grid=(B, npair),
        in_specs=[seq(wk, 0), seq(wk, 0), seq(wv, 0), seq(wk, 0), seq(wk, npair), st_spec, st_spec],
        out_specs=(seq(wv, 0), st_spec, st_spec),
        out_shape=(jax.ShapeDtypeStruct((B, L, GLA_HEADS * GLA_DV), F32), st_shape, st_shape),
        scratch_shapes=[pltpu.VMEM((GLA_PAIR, GLA_DV, GLA_DK), F32), pltpu.VMEM((GLA_PAIR, GLA_DV, GLA_DK), F32)],
        compiler_params=_cparams(2),
        name="gla",
    )(q, k, v, la, la, sf0, sb0)


HY_LANE = 128


def _sconv_kernel(p1_ref, p2_ref, p3_ref, w1_ref, w2_ref, w3_ref, b1_ref, b2_ref, b3_ref, x1_ref, x2_ref, v_ref):
    L = p1_ref.shape[1]
    row = lax.broadcasted_iota(jnp.int32, (L, HY_LANE), 0)
    for p_ref, w_ref, b_ref, o_ref in ((p1_ref, w1_ref, b1_ref, x1_ref), (p2_ref, w2_ref, b2_ref, x2_ref),
                                       (p3_ref, w3_ref, b3_ref, v_ref)):
        x = p_ref[0]
        prev = jnp.where(row == 0, 0.0, pltpu.roll(x, 1, 0))
        nxt = jnp.where(row == L - 1, 0.0, pltpu.roll(x, L - 1, 0))
        o_ref[0] = prev * w_ref[0:1, :] + x * w_ref[1:2, :] + nxt * w_ref[2:3, :] + b_ref[...]


def _sconv(p, w, b):
    B, L, _ = p.shape
    nb = HY_WIDTH // HY_LANE
    pspec = lambda g: pl.BlockSpec((1, L, HY_LANE), lambda bb, j: (bb, 0, g * nb + j))
    wspec = lambda g: pl.BlockSpec((3, HY_LANE), lambda bb, j: (0, g * nb + j))
    bspec = lambda g: pl.BlockSpec((1, HY_LANE), lambda bb, j: (0, g * nb + j))
    ospec = pl.BlockSpec((1, L, HY_LANE), lambda bb, j: (bb, 0, j))
    oshape = jax.ShapeDtypeStruct((B, L, HY_WIDTH), F32)
    b2 = b.reshape(1, -1)
    return pl.pallas_call(
        _sconv_kernel,
        grid=(B, nb),
        in_specs=[pspec(0), pspec(1), pspec(2), wspec(0), wspec(1), wspec(2), bspec(0), bspec(1), bspec(2)],
        out_specs=(ospec, ospec, ospec),
        out_shape=(oshape, oshape, oshape),
        compiler_params=_cparams(2),
        name="hy_sconv",
    )(p, p, p, w, w, w, b2, b2, b2)


def _hy_filter_kernel(feat_ref, w1_ref, b1_ref, w2_ref, b2_ref, w3_ref, delta_ref, o_ref):
    hp = lax.Precision.HIGHEST
    feat = feat_ref[...]
    h = jnp.sin(HY_SIN_FREQ * (jnp.dot(feat, w1_ref[...], precision=hp, preferred_element_type=F32) + b1_ref[...]))
    h = jnp.sin(HY_SIN_FREQ * (jnp.dot(h, w2_ref[...], precision=hp, preferred_element_type=F32) + b2_ref[...]))
    h = jnp.dot(h, w3_ref[...], precision=hp, preferred_element_type=F32)
    win = jnp.exp(-feat[:, 0:1] * delta_ref[...])
    lag0 = (pl.program_id(0) == 0) & (lax.broadcasted_iota(jnp.int32, win.shape, 0) == 0)
    win_b = jnp.where(lag0, 0.0, win)
    for o in range(HY_ORDER):
        for d, wnd in enumerate((win, win_b)):
            lo = (o * 2 + d) * HY_WIDTH
            o_ref[0, :, lo:lo + HY_WIDTH] = h[:, lo:lo + HY_WIDTH] * wnd


def _hy_filters(L, w1, b1, w2, b2, w3):
    feat, deltas = _hy_feat_consts(L)
    emb = w1.shape[0]
    w1p = jnp.zeros((HY_LANE, w1.shape[1]), F32).at[:emb].set(w1)
    tl = _pick_tile(L, 512)
    nf = w3.shape[1]
    return pl.pallas_call(
        _hy_filter_kernel,
        grid=(L // tl,),
        in_specs=[pl.BlockSpec((tl, HY_LANE), lambda i: (i, 0)), _const_spec(w1p.shape), _const_spec((1, w1.shape[1])),
                  _const_spec(w2.shape), _const_spec((1, w2.shape[1])), _const_spec(w3.shape),
                  _const_spec((1, HY_WIDTH))],
        out_specs=pl.BlockSpec((1, tl, nf), lambda i: (0, i, 0)),
        out_shape=jax.ShapeDtypeStruct((1, L, nf), F32),
        compiler_params=_cparams(1),
        name="hy_filter",
    )(jnp.asarray(feat), w1p, b1.reshape(1, -1), w2, b2.reshape(1, -1), w3, jnp.asarray(deltas))


def _dft_left_kernel(m_ref, x_ref, o_ref):
    o_ref[0] = jnp.dot(m_ref[...], x_ref[0].astype(BF16), preferred_element_type=F32).astype(o_ref.dtype)


def _dft_left(mat, x):
    B, K, W = x.shape
    R = mat.shape[0]
    tw = _pick_tile(W, 4096)
    return pl.pallas_call(
        _dft_left_kernel,
        grid=(B, W // tw),
        in_specs=[_const_spec(mat.shape), pl.BlockSpec((1, K, tw), lambda b, j: (b, 0, j))],
        out_specs=pl.BlockSpec((1, R, tw), lambda b, j: (b, 0, j)),
        out_shape=jax.ShapeDtypeStruct((B, R, W), BF16),
        compiler_params=_cparams(2),
        name="hy_dft_in",
    )(mat, x)


def _dft_spectrum_kernel(a_ref, mf_ref, o_ref, *, kb, n2):
    for s in range(kb):
        a = jnp.concatenate([a_ref[0, 0, s], a_ref[0, 1, s]], axis=0)
        x = jnp.dot(mf_ref[s], a, preferred_element_type=F32)
        xr, xi = x[:n2], x[n2:]
        for o in range(HY_ORDER):
            f = slice((2 * o) * HY_WIDTH, (2 * o + 1) * HY_WIDTH)
            b = slice((2 * o + 1) * HY_WIDTH, (2 * o + 2) * HY_WIDTH)
            o_ref[o, 0, s] = xr[:, f] + xr[:, b]
            o_ref[o, 1, s] = xi[:, f] - xi[:, b]


def _dft_spectrum(a, mf):
    _, _, n1, n2, nf = a.shape
    kb = min(4, n1)
    return pl.pallas_call(
        functools.partial(_dft_spectrum_kernel, kb=kb, n2=n2),
        grid=(n1 // kb,),
        in_specs=[pl.BlockSpec((1, 2, kb, n2, nf), lambda i: (0, 0, i, 0, 0)),
                  pl.BlockSpec((kb, 2 * n2, 2 * n2), lambda i: (i, 0, 0))],
        out_specs=pl.BlockSpec((HY_ORDER, 2, kb, n2, HY_WIDTH), lambda i: (0, 0, i, 0, 0)),
        out_shape=jax.ShapeDtypeStruct((HY_ORDER, 2, n1, n2, HY_WIDTH), F32),
        compiler_params=_cparams(1),
        name="hy_spectrum",
    )(a, mf)


def _dft_mid_kernel(a_ref, h_ref, mf_ref, mi_ref, o_ref, *, kb, n2):
    for s in range(kb):
        a = jnp.concatenate([a_ref[0, 0, s], a_ref[0, 1, s]], axis=0)
        x = jnp.dot(mf_ref[s], a, preferred_element_type=F32)
        xr, xi = x[:n2], x[n2:]
        hr, hi = h_ref[0, s], h_ref[1, s]
        y = jnp.concatenate([xr * hr - xi * hi, xr * hi + xi * hr], axis=0).astype(BF16)
        b = jnp.dot(mi_ref[s], y, preferred_element_type=F32)
        o_ref[0, 0, s] = b[:n2].astype(BF16)
        o_ref[0, 1, s] = b[n2:].astype(BF16)


def _dft_mid(a, h, mf, mi):
    B, _, n1, n2, W = a.shape
    kb = min(8, n1)
    return pl.pallas_call(
        functools.partial(_dft_mid_kernel, kb=kb, n2=n2),
        grid=(n1 // kb, B),
        in_specs=[pl.BlockSpec((1, 2, kb, n2, W), lambda i, b: (b, 0, i, 0, 0)),
                  pl.BlockSpec((2, kb, n2, W), lambda i, b: (0, i, 0, 0)),
                  pl.BlockSpec((kb, 2 * n2, 2 * n2), lambda i, b: (i, 0, 0)),
                  pl.BlockSpec((kb, 2 * n2, 2 * n2), lambda i, b: (i, 0, 0))],
        out_specs=pl.BlockSpec((1, 2, kb, n2, W), lambda i, b: (b, 0, i, 0, 0)),
        out_shape=jax.ShapeDtypeStruct((B, 2, n1, n2, W), BF16),
        compiler_params=_cparams(2),
        name="hy_dft_mid",
    )(a, h, mf, mi)


def _dft_out_kernel(m_ref, b_ref, u_ref, g_ref, bias_ref, o_ref):
    y = jnp.dot(m_ref[...], b_ref[0], preferred_element_type=F32)
    o_ref[0] = g_ref[0] * (y + u_ref[0] * bias_ref[...])


def _dft_out(mat, bp, u, gate, bias_row):
    B, K, W = bp.shape
    R = mat.shape[0]
    tw = bias_row.shape[1]
    row = lambda r: pl.BlockSpec((1, r, tw), lambda b, j: (b, 0, j))
    return pl.pallas_call(
        _dft_out_kernel,
        grid=(B, W // tw),
        in_specs=[_const_spec(mat.shape), row(K), row(R), row(R), _const_spec((1, tw))],
        out_specs=row(R),
        out_shape=jax.ShapeDtypeStruct((B, R, W), F32),
        compiler_params=_cparams(2),
        name="hy_dft_out",
    )(mat, bp, u, gate, bias_row)


@functools.lru_cache(maxsize=None)
def _hy_feat_consts(L):
    t = np.linspace(0.0, 1.0, L)
    w = 2.0 * np.pi * np.arange(L) / L
    f = np.linspace(1e-4, HY_BANDS - 1, HY_BANDS)
    fw = w[:, None] * f[None, :]
    feat = np.zeros((L, HY_LANE), np.float32)
    feat[:, :2 * HY_BANDS + 1] = np.concatenate([t[:, None], np.cos(fw), -np.sin(fw)], axis=-1)
    deltas = np.abs(np.linspace(math.log(HY_DECAY_TARGET) / HY_LONG_DECAY_PCT,
                                math.log(HY_DECAY_TARGET) / HY_SHORT_DECAY_PCT, HY_WIDTH))
    return feat, deltas.reshape(1, HY_WIDTH).astype(np.float32)


@functools.lru_cache(maxsize=None)
def _dft_consts(L):
    N = 2 * L
    n2 = 64 if L >= 2048 else 32
    n1 = N // n2
    pad_input = n1 // 2 < 16
    kin = n1 if pad_input else n1 // 2
    k1 = np.arange(n1)
    f1 = np.exp(-2j * np.pi * np.outer(k1, np.arange(kin)) / n1)
    mat_in = np.concatenate([f1.real, f1.imag], axis=0)
    a = np.arange(n2)
    f2 = np.exp(-2j * np.pi * np.outer(a, a) / n2)
    tw = np.exp(-2j * np.pi * np.outer(k1, a) / N)
    mf = f2[None, :, :] * tw[:, None, :]
    mi = np.conj(np.transpose(mf, (0, 2, 1)))
    blockform = lambda m: np.concatenate([np.concatenate([m.real, -m.imag], axis=2),
                                          np.concatenate([m.imag, m.real], axis=2)], axis=1)
    g = np.exp(2j * np.pi * np.outer(np.arange(n1 // 2), k1) / n1) / N
    mat_out = np.concatenate([g.real, -g.imag], axis=1)
    f32 = lambda m: np.ascontiguousarray(m, dtype=np.float32)
    return dict(n1=n1, n2=n2, pad_input=pad_input, mat_in=f32(mat_in), mf=f32(blockform(mf)), mi=f32(blockform(mi)),
                mat_out=f32(mat_out))


def _hyena(dp, sconv_w, sconv_b, w1, b1, w2, b2, w3, fbias):
    B, L, _ = dp.shape
    W = HY_WIDTH
    cst = _dft_consts(L)
    n1, n2 = cst["n1"], cst["n2"]
    mat_in, mat_out = jnp.asarray(cst["mat_in"], BF16), jnp.asarray(cst["mat_out"], BF16)
    mf, mi = jnp.asarray(cst["mf"], BF16), jnp.asarray(cst["mi"], BF16)

    def stage_in(u):
        b, _, c = u.shape
        if cst["pad_input"]:
            u = jnp.concatenate([u, jnp.zeros_like(u)], axis=1)
        a = _dft_left(mat_in, u.reshape(b, -1, n2 * c))
        return a.reshape(b, 2, n1, n2, c)

    x1, x2, v = _sconv(dp, sconv_w, sconv_b)
    spec = _dft_spectrum(stage_in(_hy_filters(L, w1, b1, w2, b2, w3)), mf)
    tw = _pick_tile(n2 * W, 4096)
    z = v
    for o, gate in enumerate((x1, x2)):
        bp = _dft_mid(stage_in(z), spec[o], mf, mi).reshape(B, 2 * n1, n2 * W)
        bias_row = jnp.tile(fbias[o].reshape(1, W), (1, tw // W))
        z = _dft_out(mat_out, bp, z.reshape(B, n1 // 2, n2 * W), gate.reshape(B, n1 // 2, n2 * W), bias_row)
        z = z.reshape(B, L, W)
    return z


def _swap_perm(width):
    q = width // 4
    return np.concatenate([np.arange(q, 2 * q), np.arange(0, q), np.arange(3 * q, 4 * q), np.arange(2 * q, 3 * q)])


def _rope_tables(L, width, latent):
    if not latent:
        return jnp.ones((L, width), F32), jnp.zeros((L, width), F32)
    q = width // 4
    t = jnp.arange(L, dtype=jnp.int32)
    row = (t // GRID_W).astype(F32)
    col = (t % GRID_W).astype(F32)
    freqs = ROPE_THETA ** (-jnp.arange(q, dtype=F32) / q)
    ar, ac = row[:, None] * freqs[None, :], col[:, None] * freqs[None, :]
    cos = jnp.concatenate([jnp.cos(ar), jnp.cos(ar), jnp.cos(ac), jnp.cos(ac)], axis=-1)
    sin = jnp.concatenate([-jnp.sin(ar), jnp.sin(ar), -jnp.sin(ac), jnp.sin(ac)], axis=-1)
    return cos, sin


def _tables(L, latent):
    ca, sa = _rope_tables(L, MLA_ROPE, latent)
    pad = lambda t, fill: jnp.concatenate(
        [jnp.full((L, MLA_NOPE), fill, F32), t, jnp.zeros((L, MLA_DPAD - MLA_NOPE - MLA_ROPE), F32)], axis=-1)
    cc, sc = _rope_tables(L, GQA_HEAD_DIM, latent)
    return pad(ca, 1.0), pad(sa, 0.0), cc, sc


def _prep_layer_weights(w_in, w_uq, w_ukv, gmq, gmkv, gla_w_gate, gla_b_gate, gcq, gck):
    D = w_in.shape[0]
    col = lambda i: w_in[:, IN_OFFS[i]:IN_OFFS[i + 1]]
    a_cq, a_ckv, a_kr, b_q, b_k, b_v, b_lr, b_og, c_q, c_k, c_v, d_p, gates = [col(i) for i in range(13)]
    p32, p64 = _swap_perm(MLA_ROPE), _swap_perm(GQA_HEAD_DIM)
    z = lambda n: jnp.zeros((D, n), F32)
    pad_kr = lambda t: jnp.concatenate([z(MLA_NOPE), t, z(MLA_DPAD - MLA_NOPE - MLA_ROPE)], axis=1)
    swap_heads = lambda t, n: t.reshape(D, n, GQA_HEAD_DIM)[:, :, p64].reshape(D, n * GQA_HEAD_DIM)
    w_att = jnp.concatenate([a_cq, a_ckv, pad_kr(a_kr), pad_kr(a_kr[:, p32]),
                             c_q, swap_heads(c_q, GQA_HEADS), c_k, swap_heads(c_k, GQA_KV_HEADS), c_v], axis=1)
    w_mix = jnp.concatenate([b_q, b_k, b_v, b_og, b_lr, z(128 - 2 * GLA_GATE_RANK), d_p, gates], axis=1)

    uq = w_uq.reshape(MLA_Q_LORA, MLA_HEADS, MLA_NOPE + MLA_ROPE)
    zq = jnp.zeros((MLA_Q_LORA, MLA_HEADS, MLA_DPAD - MLA_NOPE - MLA_ROPE), F32)
    uq_pad = jnp.concatenate([uq, zq], axis=-1).reshape(MLA_Q_LORA, MLA_HEADS * MLA_DPAD)
    uq_sw = jnp.concatenate([uq[..., :MLA_NOPE], uq[..., MLA_NOPE:][..., p32], zq], axis=-1)
    wuq2 = jnp.concatenate([uq_pad, uq_sw.reshape(MLA_Q_LORA, MLA_HEADS * MLA_DPAD)], axis=1)
    ukv = w_ukv.reshape(MLA_KV_LORA, MLA_HEADS, MLA_NOPE + MLA_V)
    zk = jnp.zeros((MLA_KV_LORA, MLA_HEADS, MLA_DPAD - MLA_NOPE), F32)
    uk = jnp.concatenate([ukv[..., :MLA_NOPE], zk], axis=-1).reshape(MLA_KV_LORA, MLA_HEADS * MLA_DPAD)
    uv = ukv[..., MLA_NOPE:].reshape(MLA_KV_LORA, MLA_HEADS * MLA_V)
    wukv2 = jnp.concatenate([uk, uv], axis=1)

    hk = GLA_HEADS * GLA_DK
    wgate = jnp.zeros((128, 2 * hk), F32)
    wgate = wgate.at[:GLA_GATE_RANK, :hk].set(gla_w_gate[0]).at[GLA_GATE_RANK:2 * GLA_GATE_RANK, hk:].set(gla_w_gate[1])
    return dict(w_att=w_att.astype(BF16), w_mix=w_mix.astype(BF16), wuq2=wuq2.astype(BF16), wukv2=wukv2.astype(BF16),
                gmq=gmq.reshape(1, -1), gmkv=gmkv.reshape(1, -1),
                gcq=jnp.stack([gcq, gcq[p64]]), gck=jnp.stack([gck, gck[p64]]),
                wgate=wgate.astype(BF16), bgate=gla_b_gate.reshape(1, 2 * hk))


def kernel(x, c, ctx, c_ctx, ada_w, ada_b, norm_g, ffn_w_gate, ffn_w_up, ffn_w_down, w_in, mla_q_norm_g, mla_w_uq, mla_kv_norm_g, mla_w_ukv, gla_w_gate, gla_b_gate, gla_norm_g, gqa_q_norm_g, gqa_k_norm_g, hy_sconv_w, hy_sconv_b, hy_filt_w1, hy_filt_b1, hy_filt_w2, hy_filt_b2, hy_filt_w3, hy_filt_bias, w_branch, w_out, final_g):
    B, L, D = x.shape
    Lc = ctx.shape[1]
    depth = ada_w.shape[0]
    tabs_lat = _tables(L, True)
    tabs_ctx = _tables(Lc, False)
    cvec = jnp.concatenate([c, c_ctx[None]], axis=0)
    xc = ctx
    for l in range(depth):
        need_ctx = l < depth - 1
        mod = _ada(cvec, ada_w[l], ada_b[l]).reshape(B + 1, N_ADA, D)
        mod_lat = mod[:B]
        mod_ctx = jnp.broadcast_to(mod[B:], (B, N_ADA, D))
        wg, wu, wd = ffn_w_gate[l].astype(BF16), ffn_w_up[l].astype(BF16), ffn_w_down[l].astype(BF16)
        wp = _prep_layer_weights(w_in[l], mla_w_uq[l], mla_w_ukv[l], mla_q_norm_g[l], mla_kv_norm_g[l],
                                 gla_w_gate[l], gla_b_gate[l], gqa_q_norm_g[l], gqa_k_norm_g[l])
        wb, wo = w_branch[l].astype(BF16), w_out[l].astype(BF16)

        x = _ffn(x, mod_lat[:, 0:3], norm_g[l, 0], wg[0], wu[0], wd[0])
        xc = _ffn(xc, mod_ctx[:, 0:3], norm_g[l, 0], wg[0], wu[0], wd[0])

        qa, ka, va, qc, kc, vc = _attn_prep(x, mod_lat[:, 3:6], norm_g[l, 1], wp, tabs_lat)
        qa_c, ka_c, va_c, qc_c, kc_c, vc_c = _attn_prep(xc, mod_ctx[:, 3:6], norm_g[l, 1], wp, tabs_ctx)
        gq, gk, gv, la, og, dp, gates = _mix_prep(x, mod_lat[:, 3:6], norm_g[l, 1], wp)
        gq_c, gk_c, gv_c, la_c, og_c, dp_c, gates_c = _mix_prep(xc, mod_ctx[:, 3:6], norm_g[l, 1], wp)

        cat = lambda a, b: jnp.concatenate([a, b], axis=2)
        o_a = _attend(qa, cat(ka, ka_c), cat(va, va_c))
        qc_g = qc.reshape(B, GQA_KV_HEADS, (GQA_HEADS // GQA_KV_HEADS) * L, GQA_HEAD_DIM)
        o_c = _attend(qc_g, cat(kc, kc_c), cat(vc, vc_c)).reshape(B, GQA_HEADS, L, GQA_HEAD_DIM)

        s0 = jnp.zeros((B, GLA_HEADS, GLA_DV, GLA_DK), F32)
        ob_c, s_f, s_b = _gla(gq_c, gk_c, gv_c, la_c, s0, s0)
        o_b, _, _ = _gla(gq, gk, gv, la, s_f, s_b)

        hy = (hy_sconv_w[l], hy_sconv_b[l], hy_filt_w1[l], hy_filt_b1[l], hy_filt_w2[l], hy_filt_b2[l],
              hy_filt_w3[l], hy_filt_bias[l])
        o_d = _hyena(dp, *hy)

        x = _merge(x, mod_lat[:, 5:6], o_a, o_b, og, gla_norm_g[l], o_c, o_d, gates, wb, wo)
        last = l == depth - 1
        x = _ffn(x, mod_lat[:, 6:9], norm_g[l, 2], wg[1], wu[1], wd[1], final_g=final_g if last else None)
        if need_ctx:
            o_ac = _attend(qa_c, ka_c, va_c)
            qcc_g = qc_c.reshape(B, GQA_KV_HEADS, (GQA_HEADS // GQA_KV_HEADS) * Lc, GQA_HEAD_DIM)
            o_cc = _attend(qcc_g, kc_c, vc_c).reshape(B, GQA_HEADS, Lc, GQA_HEAD_DIM)
            o_dc = _hyena(dp_c, *hy)
            xc = _merge(xc, mod_ctx[:, 5:6], o_ac, ob_c, og_c, gla_norm_g[l], o_cc, o_dc, gates_c, wb, wo)
            xc = _ffn(xc, mod_ctx[:, 6:9], norm_g[l, 2], wg[1], wu[1], wd[1])
    return x
```

```python
import functools
import math

import jax
import jax.numpy as jnp
import numpy as np
from jax import lax
from jax.experimental import pallas as pl
from jax.experimental.pallas import tpu as pltpu

F32 = jnp.float32
BF16 = jnp.bfloat16

D_MODEL = 1024
GRID_W = 64
N_ADA = 9
FFN_DIM = 2816
RMS_EPS = 1e-6
ROPE_THETA = 10000.0
MLA_HEADS = 8
MLA_NOPE = 64
MLA_ROPE = 32
MLA_V = 64
MLA_Q_LORA = 256
MLA_KV_LORA = 128
MLA_SCALE = (MLA_NOPE + MLA_ROPE) ** -0.5
MLA_DPAD = 128
GLA_HEADS = 4
GLA_DK = 64
GLA_DV = 128
GLA_GATE_RANK = 16
GLA_GATE_TEMP = 16.0
GLA_CHUNK = 64
GQA_HEADS = 8
GQA_KV_HEADS = 2
GQA_HEAD_DIM = 64
GQA_SCALE = GQA_HEAD_DIM ** -0.5
HY_WIDTH = 512
HY_ORDER = 2
HY_BANDS = 16
HY_SIN_FREQ = 1.0
HY_DECAY_TARGET = 1e-2
HY_SHORT_DECAY_PCT = 0.3
HY_LONG_DECAY_PCT = 1.5
N_BRANCH = 4
BRANCH_WIDTH = 512
LOG2E = math.log2(math.e)

IN_SIZES = (MLA_Q_LORA, MLA_KV_LORA, MLA_ROPE,
            GLA_HEADS * GLA_DK, GLA_HEADS * GLA_DK, GLA_HEADS * GLA_DV, 2 * GLA_GATE_RANK, GLA_HEADS * GLA_DV,
            GQA_HEADS * GQA_HEAD_DIM, GQA_KV_HEADS * GQA_HEAD_DIM, GQA_KV_HEADS * GQA_HEAD_DIM,
            (HY_ORDER + 1) * HY_WIDTH,
            N_BRANCH * D_MODEL)
IN_OFFS = tuple(int(v) for v in np.cumsum((0,) + IN_SIZES))

VMEM_LIMIT_BYTES = 56 * 1024 * 1024


def _cparams(n_axes):
    return pltpu.CompilerParams(dimension_semantics=("arbitrary",) * n_axes,
                                vmem_limit_bytes=VMEM_LIMIT_BYTES)


def _const_spec(shape):
    nd = len(shape)
    return pl.BlockSpec(shape, lambda *_: (0,) * nd, pipeline_mode=pl.Buffered(1))


def _pick_tile(n, pref):
    t = min(n, pref)
    while n % t:
        t //= 2
    return t


def _rms(x):
    return x * lax.rsqrt(jnp.mean(x * x, axis=-1, keepdims=True) + RMS_EPS)


def _norm_mod(x, g, shift, scale):
    return (_rms(x) * g) * (1.0 + scale) + shift


def _ada_kernel(s_ref, w_ref, b_ref, o_ref):
    s = s_ref[...]
    s = s * jax.nn.sigmoid(s)
    o_ref[...] = jnp.dot(s.astype(BF16), w_ref[...].astype(BF16), preferred_element_type=F32) + b_ref[...]


def _ada(cvec, w, b):
    rows, d = cvec.shape
    n = w.shape[1]
    tn = _pick_tile(n, 1152)
    return pl.pallas_call(
        _ada_kernel,
        grid=(n // tn,),
        in_specs=[pl.BlockSpec((rows, d), lambda j: (0, 0)),
                  pl.BlockSpec((d, tn), lambda j: (0, j)),
                  pl.BlockSpec((1, tn), lambda j: (0, j))],
        out_specs=pl.BlockSpec((rows, tn), lambda j: (0, j)),
        out_shape=jax.ShapeDtypeStruct((rows, n), F32),
        compiler_params=_cparams(1),
        name="ada_mod",
    )(cvec, w, b.reshape(1, n))


def _ffn_kernel(x_ref, mod_ref, g_ref, wg_ref, wu_ref, wd_ref, fg_ref, o_ref, *, final_norm):
    x = x_ref[0]
    mod = mod_ref[0]
    h = _norm_mod(x, g_ref[...], mod[0:1], mod[1:2]).astype(BF16)
    a = jnp.dot(h, wg_ref[...], preferred_element_type=F32)
    b = jnp.dot(h, wu_ref[...], preferred_element_type=F32)
    act = (a * jax.nn.sigmoid(a) * b).astype(BF16)
    y = jnp.dot(act, wd_ref[...], preferred_element_type=F32)
    out = x + (0.5 * mod[2:3]) * y
    if final_norm:
        out = _rms(out) * fg_ref[...]
    o_ref[0] = out


def _ffn(x, mod3, g, wg, wu, wd, final_g=None):
    B, L, D = x.shape
    F = wg.shape[1]
    tm = _pick_tile(L, 512)
    final_norm = final_g is not None
    fg = (final_g if final_norm else jnp.ones((D,), F32)).reshape(1, D)
    return pl.pallas_call(
        functools.partial(_ffn_kernel, final_norm=final_norm),
        grid=(B, L // tm),
        in_specs=[pl.BlockSpec((1, tm, D), lambda b, i: (b, i, 0)),
                  pl.BlockSpec((1, 3, D), lambda b, i: (b, 0, 0)),
                  _const_spec((1, D)), _const_spec((D, F)), _const_spec((D, F)), _const_spec((F, D)),
                  _const_spec((1, D))],
        out_specs=pl.BlockSpec((1, tm, D), lambda b, i: (b, i, 0)),
        out_shape=jax.ShapeDtypeStruct((B, L, D), F32),
        compiler_params=_cparams(2),
        name="ffn",
    )(x, mod3, g.reshape(1, D), wg, wu, wd, fg)


QA_SCALE = MLA_SCALE * LOG2E
QC_SCALE = GQA_SCALE * LOG2E
_A_CQ, _A_CKV, _A_KR, _A_KRS, _C_Q, _C_QS, _C_K, _C_KS, _C_V, _ATT_COLS = (
    0, 256, 384, 512, 640, 1152, 1664, 1792, 1920, 2048)


def _attn_prep_kernel(x_ref, mod_ref, g_ref, w_ref, wuq_ref, wukv_ref, gmq_ref, gmkv_ref, gcq_ref, gck_ref,
                      cosa_ref, sina_ref, cosc_ref, sinc_ref,
                      qa_ref, ka_ref, va_ref, qc_ref, kc_ref, vc_ref):
    x = x_ref[0]
    mod = mod_ref[0]
    h = _norm_mod(x, g_ref[...], mod[0:1], mod[1:2]).astype(BF16)
    p = jnp.dot(h, w_ref[...], preferred_element_type=F32)
    cosa, sina = cosa_ref[...], sina_ref[...]
    cosc, sinc = cosc_ref[...], sinc_ref[...]

    kr = p[:, _A_KR:_A_KR + 128] * cosa + p[:, _A_KRS:_A_KRS + 128] * sina
    nq = (_rms(p[:, _A_CQ:_A_CQ + MLA_Q_LORA]) * gmq_ref[...]).astype(BF16)
    q2 = jnp.dot(nq, wuq_ref[...], preferred_element_type=F32)
    nkv = (_rms(p[:, _A_CKV:_A_CKV + MLA_KV_LORA]) * gmkv_ref[...]).astype(BF16)
    kv = jnp.dot(nkv, wukv_ref[...], preferred_element_type=F32)
    hq = MLA_HEADS * MLA_DPAD
    for hd in range(MLA_HEADS):
        lo = hd * MLA_DPAD
        qh = q2[:, lo:lo + MLA_DPAD] * cosa + q2[:, hq + lo:hq + lo + MLA_DPAD] * sina
        qa_ref[0, hd] = (qh * QA_SCALE).astype(BF16)
        ka_ref[0, hd] = (kv[:, lo:lo + MLA_DPAD] + kr).astype(BF16)
        va_ref[0, hd] = kv[:, hq + hd * MLA_V:hq + (hd + 1) * MLA_V].astype(BF16)

    def normed_rot(t, ts, gains):
        inv = lax.rsqrt(jnp.mean(t * t, axis=-1, keepdims=True) + RMS_EPS)
        return ((t * inv) * gains[0:1]) * cosc + ((ts * inv) * gains[1:2]) * sinc

    gq, gk = gcq_ref[...], gck_ref[...]
    dh = GQA_HEAD_DIM
    for hd in range(GQA_HEADS):
        qh = normed_rot(p[:, _C_Q + hd * dh:_C_Q + (hd + 1) * dh], p[:, _C_QS + hd * dh:_C_QS + (hd + 1) * dh], gq)
        qc_ref[0, hd] = (qh * QC_SCALE).astype(BF16)
    for hd in range(GQA_KV_HEADS):
        kh = normed_rot(p[:, _C_K + hd * dh:_C_K + (hd + 1) * dh], p[:, _C_KS + hd * dh:_C_KS + (hd + 1) * dh], gk)
        kc_ref[0, hd] = kh.astype(BF16)
        vc_ref[0, hd] = p[:, _C_V + hd * dh:_C_V + (hd + 1) * dh].astype(BF16)


def _attn_prep(x, mod3, g, wp, tabs):
    B, L, D = x.shape
    tm = _pick_tile(L, 512)
    cosa, sina, cosc, sinc = tabs
    tok = lambda w: pl.BlockSpec((1, tm, w), lambda b, i: (b, i, 0))
    heads = lambda n, w: pl.BlockSpec((1, n, tm, w), lambda b, i: (b, 0, i, 0))
    tab = lambda w: pl.BlockSpec((tm, w), lambda b, i: (i, 0))
    out_shape = (jax.ShapeDtypeStruct((B, MLA_HEADS, L, MLA_DPAD), BF16),
                 jax.ShapeDtypeStruct((B, MLA_HEADS, L, MLA_DPAD), BF16),
                 jax.ShapeDtypeStruct((B, MLA_HEADS, L, MLA_V), BF16),
                 jax.ShapeDtypeStruct((B, GQA_HEADS, L, GQA_HEAD_DIM), BF16),
                 jax.ShapeDtypeStruct((B, GQA_KV_HEADS, L, GQA_HEAD_DIM), BF16),
                 jax.ShapeDtypeStruct((B, GQA_KV_HEADS, L, GQA_HEAD_DIM), BF16))
    return pl.pallas_call(
        _attn_prep_kernel,
        grid=(B, L // tm),
        in_specs=[tok(D), pl.BlockSpec((1, 3, D), lambda b, i: (b, 0, 0)), _const_spec((1, D)),
                  _const_spec(wp["w_att"].shape), _const_spec(wp["wuq2"].shape), _const_spec(wp["wukv2"].shape),
                  _const_spec((1, MLA_Q_LORA)), _const_spec((1, MLA_KV_LORA)),
                  _const_spec((2, GQA_HEAD_DIM)), _const_spec((2, GQA_HEAD_DIM)),
                  tab(MLA_DPAD), tab(MLA_DPAD), tab(GQA_HEAD_DIM), tab(GQA_HEAD_DIM)],
        out_specs=(heads(MLA_HEADS, MLA_DPAD), heads(MLA_HEADS, MLA_DPAD), heads(MLA_HEADS, MLA_V),
                   heads(GQA_HEADS, GQA_HEAD_DIM), heads(GQA_KV_HEADS, GQA_HEAD_DIM),
                   heads(GQA_KV_HEADS, GQA_HEAD_DIM)),
        out_shape=out_shape,
        compiler_params=_cparams(2),
        name="attn_prep",
    )(x, mod3, g.reshape(1, D), wp["w_att"], wp["wuq2"], wp["wukv2"], wp["gmq"], wp["gmkv"], wp["gcq"], wp["gck"],
      cosa, sina, cosc, sinc)


_B_Q, _B_K, _B_V, _B_OG, _B_LR, _D_P, _GATES, _MIX_COLS = (0, 256, 512, 1024, 1536, 1664, 3200, 7296)


def _mix_prep_kernel(x_ref, mod_ref, g_ref, w_ref, wgate_ref, bgate_ref,
                     gq_ref, gk_ref, gv_ref, la_ref, og_ref, dp_ref, gates_ref):
    x = x_ref[0]
    mod = mod_ref[0]
    h = _norm_mod(x, g_ref[...], mod[0:1], mod[1:2]).astype(BF16)
    pg = jnp.dot(h, w_ref[:, _B_Q:_D_P], preferred_element_type=F32)
    gq_ref[0] = pg[:, _B_Q:_B_K] * (GLA_DK ** -0.5)
    gk_ref[0] = pg[:, _B_K:_B_V]
    gv_ref[0] = pg[:, _B_V:_B_OG]
    og_ref[0] = pg[:, _B_OG:_B_LR]
    pre = jnp.dot(pg[:, _B_LR:_D_P].astype(BF16), wgate_ref[...], preferred_element_type=F32) + bgate_ref[...]
    la = (jnp.minimum(pre, 0.0) - jnp.log1p(jnp.exp(-jnp.abs(pre)))) * (1.0 / GLA_GATE_TEMP)
    C = GLA_CHUNK
    ii = lax.broadcasted_iota(jnp.int32, (C, C), 0)
    jj = lax.broadcasted_iota(jnp.int32, (C, C), 1)
    tri_low = jnp.where(ii >= jj, 1.0, 0.0).astype(BF16)
    tri_up = jnp.where(ii <= jj, 1.0, 0.0).astype(BF16)
    hi = la.astype(BF16)
    lo = (la - hi.astype(F32)).astype(BF16)
    hk = GLA_HEADS * GLA_DK
    for c in range(la.shape[0] // C):
        rs = slice(c * C, (c + 1) * C)
        la_ref[0, rs, :hk] = (jnp.dot(tri_low, hi[rs, :hk], preferred_element_type=F32)
                              + jnp.dot(tri_low, lo[rs, :hk], preferred_element_type=F32))
        la_ref[0, rs, hk:] = (jnp.dot(tri_up, hi[rs, hk:], preferred_element_type=F32)
                              + jnp.dot(tri_up, lo[rs, hk:], preferred_element_type=F32))
    dp_ref[0] = jnp.dot(h, w_ref[:, _D_P:_GATES], preferred_element_type=F32)
    gates_ref[0] = jax.nn.sigmoid(jnp.dot(h, w_ref[:, _GATES:_MIX_COLS], preferred_element_type=F32)).astype(BF16)


def _mix_prep(x, mod3, g, wp):
    B, L, D = x.shape
    tm = _pick_tile(L, 256)
    tok = lambda w: pl.BlockSpec((1, tm, w), lambda b, i: (b, i, 0))
    widths = (256, 256, 512, 512, 512, 1536, 4096)
    dtypes = (F32, F32, F32, F32, F32, F32, BF16)
    return pl.pallas_call(
        _mix_prep_kernel,
        grid=(B, L // tm),
        in_specs=[tok(D), pl.BlockSpec((1, 3, D), lambda b, i: (b, 0, 0)), _const_spec((1, D)),
                  _const_spec(wp["w_mix"].shape), _const_spec(wp["wgate"].shape), _const_spec((1, 512))],
        out_specs=tuple(tok(w) for w in widths),
        out_shape=tuple(jax.ShapeDtypeStruct((B, L, w), dt) for w, dt in zip(widths, dtypes)),
        compiler_params=_cparams(2),
        name="mix_prep",
    )(x, mod3, g.reshape(1, D), wp["w_mix"], wp["wgate"], wp["bgate"])


ATT_LANE = 128
ATT_TK = 256


def _lane_fold(x, op):
    acc = x[:, :ATT_LANE]
    for j in range(1, x.shape[1] // ATT_LANE):
        acc = op(acc, x[:, j * ATT_LANE:(j + 1) * ATT_LANE])
    return acc


def _attn_kernel(q_ref, k_ref, v_ref, o_ref, *, tk):
    q = q_ref[0, 0]
    nck = k_ref.shape[2] // tk
    nt = (((1,), (1,)), ((), ()))
    m = ls = acc = None
    for c in range(nck):
        s = lax.dot_general(q, k_ref[0, 0, c * tk:(c + 1) * tk, :], nt, preferred_element_type=F32)
        cm = jnp.max(s, axis=-1, keepdims=True)
        m_new = cm if c == 0 else jnp.maximum(m, cm)
        p = jnp.exp2(s - m_new)
        cl = _lane_fold(p, jnp.add)
        pv = jnp.dot(p.astype(BF16), v_ref[0, 0, c * tk:(c + 1) * tk, :], preferred_element_type=F32)
        if c == 0:
            ls, acc = cl, pv
        else:
            alpha = jnp.exp2(m - m_new)
            ls = alpha * ls + cl
            acc = alpha * acc + pv
        m = m_new
    l = jnp.sum(ls, axis=-1, keepdims=True)
    o_ref[0, 0] = (acc / l).astype(o_ref.dtype)


def _attend(q, k, v, group=1):
    B, H, Lq, d = q.shape
    Lk, dv = v.shape[2], v.shape[3]
    tq = _pick_tile(Lq, 512)
    tk = _pick_tile(Lk, ATT_TK)
    return pl.pallas_call(
        functools.partial(_attn_kernel, tk=tk),
        grid=(B, H, Lq // tq),
        in_specs=[pl.BlockSpec((1, 1, tq, d), lambda b, h, i: (b, h, i, 0)),
                  pl.BlockSpec((1, 1, Lk, d), lambda b, h, i: (b, h // group, 0, 0)),
                  pl.BlockSpec((1, 1, Lk, dv), lambda b, h, i: (b, h // group, 0, 0))],
        out_specs=pl.BlockSpec((1, 1, tq, dv), lambda b, h, i: (b, h, i, 0)),
        out_shape=jax.ShapeDtypeStruct((B, H, Lq, dv), BF16),
        compiler_params=_cparams(3),
        name="attend",
    )(q, k, v)


def _merge_kernel(x_ref, mod_ref, oa_ref, ob_ref, og_ref, gn_ref, oc_ref, od_ref, gates_ref, wb_ref, wo_ref, o_ref):
    oa = jnp.concatenate([oa_ref[0, hd] for hd in range(MLA_HEADS)], axis=-1)
    oc = jnp.concatenate([oc_ref[0, hd] for hd in range(GQA_HEADS)], axis=-1)
    ob = ob_ref[0]
    ob = jnp.concatenate([_rms(ob[:, hd * GLA_DV:(hd + 1) * GLA_DV]) * gn_ref[...] for hd in range(GLA_HEADS)], axis=-1)
    og = og_ref[0]
    ob = ob * (og * jax.nn.sigmoid(og))
    branches = (oa, ob.astype(BF16), oc, od_ref[0].astype(BF16))
    y = None
    for i, o in enumerate(branches):
        gate = gates_ref[0, :, i * D_MODEL:(i + 1) * D_MODEL].astype(F32)
        t = gate * jnp.dot(o, wb_ref[i], preferred_element_type=F32)
        y = t if y is None else y + t
    z = jnp.dot(y.astype(BF16), wo_ref[...], preferred_element_type=F32)
    o_ref[0] = x_ref[0] + mod_ref[0] * z


def _merge(x, g2, oa, ob, og, gla_g, oc, od, gates, wb, wo):
    B, L, D = x.shape
    tm = _pick_tile(L, 512)
    tok = lambda w: pl.BlockSpec((1, tm, w), lambda b, i: (b, i, 0))
    heads = lambda n, w: pl.BlockSpec((1, n, tm, w), lambda b, i: (b, 0, i, 0))
    return pl.pallas_call(
        _merge_kernel,
        grid=(B, L // tm),
        in_specs=[tok(D), pl.BlockSpec((1, 1, D), lambda b, i: (b, 0, 0)),
                  heads(MLA_HEADS, MLA_V), tok(BRANCH_WIDTH), tok(BRANCH_WIDTH), _const_spec((1, GLA_DV)),
                  heads(GQA_HEADS, GQA_HEAD_DIM), tok(BRANCH_WIDTH),
                  tok(N_BRANCH * D), _const_spec(wb.shape), _const_spec(wo.shape)],
        out_specs=tok(D),
        out_shape=jax.ShapeDtypeStruct((B, L, D), F32),
        compiler_params=_cparams(2),
        name="merge",
    )(x, g2, oa, ob, og, gla_g.reshape(1, GLA_DV), oc, od, gates, wb, wo)


GLA_PAIR = 2


def _gla_kernel(q_ref, k_ref, v_ref, laf_ref, lab_ref, sf0_ref, sb0_ref, o_ref, sf_ref, sb_ref, stf, stb, *, n_chunks):
    C = GLA_CHUNK
    ii = lax.broadcasted_iota(jnp.int32, (C, C), 0)
    jj = lax.broadcasted_iota(jnp.int32, (C, C), 1)
    low, up = ii >= jj, ii <= jj
    nt = (((1,), (1,)), ((), ()))
    stf[...] = sf0_ref[0]
    stb[...] = sb0_ref[0]

    def one_direction(n, mask, bc_ref, st, last_row, accumulate):
        rows = pl.ds(pl.multiple_of(n * C, C), C)
        q, k, v, bc = q_ref[0, rows, :], k_ref[0, rows, :], v_ref[0, rows, :], bc_ref[0, rows, :]
        b_last = bc[last_row:last_row + 1, :]
        q_dec = (q * jnp.exp(bc)).astype(BF16)
        k_inv = (k * jnp.exp(-bc)).astype(BF16)
        k_end = (k * jnp.exp(b_last - bc)).astype(BF16)
        decay = jnp.exp(b_last)
        outs = []
        for h in range(GLA_PAIR):
            ks = slice(h * GLA_DK, (h + 1) * GLA_DK)
            vh = v[:, h * GLA_DV:(h + 1) * GLA_DV]
            a = lax.dot_general(q_dec[:, ks], k_inv[:, ks], nt, preferred_element_type=F32)
            a = jnp.where(mask, a, 0.0).astype(BF16)
            s_t = st[h]
            outs.append(jnp.dot(a, vh.astype(BF16), preferred_element_type=F32)
                        + lax.dot_general(q_dec[:, ks], s_t.astype(BF16), nt, preferred_element_type=F32))
            st[h] = s_t * decay[:, ks] + jnp.dot(vh.T.astype(BF16), k_end[:, ks], preferred_element_type=F32)
        o = jnp.concatenate(outs, axis=-1)
        if accumulate:
            o_ref[0, rows, :] = o_ref[0, rows, :] + o
        else:
            o_ref[0, rows, :] = o

    def sweep(accumulate):
        def body(n, carry):
            one_direction(n, low, laf_ref, stf, C - 1, accumulate)
            one_direction(n_chunks - 1 - n, up, lab_ref, stb, 0, accumulate)
            return carry
        return body

    half = n_chunks // 2
    lax.fori_loop(0, half, sweep(False), 0, unroll=2)
    lax.fori_loop(half, n_chunks, sweep(True), 0, unroll=2)
    sf_ref[0] = stf[...]
    sb_ref[0] = stb[...]


def _gla(q, k, v, la, sf0, sb0):
    B, L, _ = q.shape
    n_chunks = L // GLA_CHUNK
    assert L % GLA_CHUNK == 0 and n_chunks % 2 == 0
    wk, wv, npair = GLA_PAIR * GLA_DK, GLA_PAIR * GLA_DV, GLA_HEADS // GLA_PAIR
    seq = lambda w, off: pl.BlockSpec((1, L, w), lambda b, p: (b, 0, p + off))
    st_spec = pl.BlockSpec((1, GLA_PAIR, GLA_DV, GLA_DK), lambda b, p: (b, p, 0, 0))
    st_shape = jax.ShapeDtypeStruct((B, GLA_HEADS, GLA_DV, GLA_DK), F32)
    return pl.pallas_call(
        functools.partial(_gla_kernel, n_chunks=n_chunks),
        grid=(B, npair),
        in_specs=[seq(wk, 0), seq(wk, 0), seq(wv, 0), seq(wk, 0), seq(wk, npair), st_spec, st_spec],
        out_specs=(seq(wv, 0), st_spec, st_spec),
        out_shape=(jax.ShapeDtypeStruct((B, L, GLA_HEADS * GLA_DV), F32), st_shape, st_shape),
        scratch_shapes=[pltpu.VMEM((GLA_PAIR, GLA_DV, GLA_DK), F32), pltpu.VMEM((GLA_PAIR, GLA_DV, GLA_DK), F32)],
        compiler_params=_cparams(2),
        name="gla",
    )(q, k, v, la, la, sf0, sb0)


HY_LANE = 128


def _sconv_kernel(p1_ref, p2_ref, p3_ref, w1_ref, w2_ref, w3_ref, b1_ref, b2_ref, b3_ref, x1_ref, x2_ref, v_ref):
    L = p1_ref.shape[1]
    row = lax.broadcasted_iota(jnp.int32, (L, HY_LANE), 0)
    for p_ref, w_ref, b_ref, o_ref in ((p1_ref, w1_ref, b1_ref, x1_ref), (p2_ref, w2_ref, b2_ref, x2_ref),
                                       (p3_ref, w3_ref, b3_ref, v_ref)):
        x = p_ref[0]
        prev = jnp.where(row == 0, 0.0, pltpu.roll(x, 1, 0))
        nxt = jnp.where(row == L - 1, 0.0, pltpu.roll(x, L - 1, 0))
        o_ref[0] = prev * w_ref[0:1, :] + x * w_ref[1:2, :] + nxt * w_ref[2:3, :] + b_ref[...]


def _sconv(p, w, b):
    B, L, _ = p.shape
    nb = HY_WIDTH // HY_LANE
    pspec = lambda g: pl.BlockSpec((1, L, HY_LANE), lambda bb, j: (bb, 0, g * nb + j))
    wspec = lambda g: pl.BlockSpec((3, HY_LANE), lambda bb, j: (0, g * nb + j))
    bspec = lambda g: pl.BlockSpec((1, HY_LANE), lambda bb, j: (0, g * nb + j))
    ospec = pl.BlockSpec((1, L, HY_LANE), lambda bb, j: (bb, 0, j))
    oshape = jax.ShapeDtypeStruct((B, L, HY_WIDTH), F32)
    b2 = b.reshape(1, -1)
    return pl.pallas_call(
        _sconv_kernel,
        grid=(B, nb),
        in_specs=[pspec(0), pspec(1), pspec(2), wspec(0), wspec(1), wspec(2), bspec(0), bspec(1), bspec(2)],
        out_specs=(ospec, ospec, ospec),
        out_shape=(oshape, oshape, oshape),
        compiler_params=_cparams(2),
        name="hy_sconv",
    )(p, p, p, w, w, w, b2, b2, b2)


def _hy_filter_kernel(feat_ref, w1_ref, b1_ref, w2_ref, b2_ref, w3_ref, delta_ref, o_ref):
    hp = lax.Precision.HIGHEST
    feat = feat_ref[...]
    h = jnp.sin(HY_SIN_FREQ * (jnp.dot(feat, w1_ref[...], precision=hp, preferred_element_type=F32) + b1_ref[...]))
    h = jnp.sin(HY_SIN_FREQ * (jnp.dot(h, w2_ref[...], precision=hp, preferred_element_type=F32) + b2_ref[...]))
    h = jnp.dot(h, w3_ref[...], precision=hp, preferred_element_type=F32)
    win = jnp.exp(-feat[:, 0:1] * delta_ref[...])
    lag0 = (pl.program_id(0) == 0) & (lax.broadcasted_iota(jnp.int32, win.shape, 0) == 0)
    win_b = jnp.where(lag0, 0.0, win)
    for o in range(HY_ORDER):
        for d, wnd in enumerate((win, win_b)):
            lo = (o * 2 + d) * HY_WIDTH
            o_ref[0, :, lo:lo + HY_WIDTH] = h[:, lo:lo + HY_WIDTH] * wnd


def _hy_filters(L, w1, b1, w2, b2, w3):
    feat, deltas = _hy_feat_consts(L)
    emb = w1.shape[0]
    w1p = jnp.zeros((HY_LANE, w1.shape[1]), F32).at[:emb].set(w1)
    tl = _pick_tile(L, 512)
    nf = w3.shape[1]
    return pl.pallas_call(
        _hy_filter_kernel,
        grid=(L // tl,),
        in_specs=[pl.BlockSpec((tl, HY_LANE), lambda i: (i, 0)), _const_spec(w1p.shape), _const_spec((1, w1.shape[1])),
                  _const_spec(w2.shape), _const_spec((1, w2.shape[1])), _const_spec(w3.shape),
                  _const_spec((1, HY_WIDTH))],
        out_specs=pl.BlockSpec((1, tl, nf), lambda i: (0, i, 0)),
        out_shape=jax.ShapeDtypeStruct((1, L, nf), F32),
        compiler_params=_cparams(1),
        name="hy_filter",
    )(jnp.asarray(feat), w1p, b1.reshape(1, -1), w2, b2.reshape(1, -1), w3, jnp.asarray(deltas))


@functools.lru_cache(maxsize=None)
def _hy_feat_consts(L):
    t = np.linspace(0.0, 1.0, L)
    w = 2.0 * np.pi * np.arange(L) / L
    f = np.linspace(1e-4, HY_BANDS - 1, HY_BANDS)
    fw = w[:, None] * f[None, :]
    feat = np.zeros((L, HY_LANE), np.float32)
    feat[:, :2 * HY_BANDS + 1] = np.concatenate([t[:, None], np.cos(fw), -np.sin(fw)], axis=-1)
    deltas = np.abs(np.linspace(math.log(HY_DECAY_TARGET) / HY_LONG_DECAY_PCT,
                                math.log(HY_DECAY_TARGET) / HY_SHORT_DECAY_PCT, HY_WIDTH))
    return feat, deltas.reshape(1, HY_WIDTH).astype(np.float32)


HY_GROUP_ROWS = 256


@functools.lru_cache(maxsize=None)
def _cm_consts(L):
    assert L % HY_LANE == 0 and HY_GROUP_ROWS % (L // HY_LANE) == 0
    N = 2 * L
    mh = L // HY_LANE
    m2n = 2 * mh
    g = HY_GROUP_ROWS // mh
    kb, m2 = np.arange(m2n), np.arange(mh)
    c = np.exp(-2j * np.pi * np.outer(kb, m2) / m2n)
    eye = np.eye(g)
    lhs_r = np.concatenate([np.kron(eye, c.real), np.kron(eye, c.imag)], axis=0)
    m1 = np.arange(HY_LANE)
    tw = np.tile(np.exp(-2j * np.pi * np.outer(kb, m1) / N), (g, 1))
    f = np.exp(-2j * np.pi * np.outer(m1, m1) / HY_LANE)
    rm = np.block([[f.real, f.imag], [-f.imag, f.real]])
    rm_inv = np.block([[f.real, -f.imag], [f.imag, f.real]])
    gi = np.exp(2j * np.pi * np.outer(m2, kb) / m2n) / N
    lhs_rinv = np.concatenate([np.kron(eye, gi.real), -np.kron(eye, gi.imag)], axis=1)
    f32 = lambda m: np.ascontiguousarray(m, dtype=np.float32)
    return dict(mh=mh, m2n=m2n, g=g, lhs_r=f32(lhs_r), lhs_rinv=f32(lhs_rinv), rm=f32(rm), rm_inv=f32(rm_inv),
                tw=f32(np.stack([tw.real, tw.imag])))


def _cm_forward(za, zb, lhs_r, tw_r, tw_i, rm):
    half = lhs_r.shape[0] // 2
    b = jnp.dot(lhs_r, jnp.concatenate([za, zb], axis=1).astype(BF16), preferred_element_type=F32)
    out = []
    for lanes in (slice(0, HY_LANE), slice(HY_LANE, 2 * HY_LANE)):
        br, bi = b[:half, lanes], b[half:, lanes]
        bp = jnp.concatenate([br * tw_r - bi * tw_i, br * tw_i + bi * tw_r], axis=1).astype(BF16)
        out.append(jnp.dot(bp, rm, preferred_element_type=F32))
    return out


def _cm_conv_kernel(v_ref, x1_ref, x2_ref, bias_ref, h_ref, lr_ref, lri_ref, rm_ref, rmi_ref, tw_ref, o_ref, *, groups):
    rows = HY_GROUP_ROWS
    tw_r, tw_i = tw_ref[0], tw_ref[1]
    lhs_r, lhs_rinv, rm, rm_inv = lr_ref[...], lri_ref[...], rm_ref[...], rmi_ref[...]
    hrows = tw_r.shape[0]
    for pair in range(groups // 2):
        rss = [slice(gi * rows, (gi + 1) * rows) for gi in (2 * pair, 2 * pair + 1)]
        hss = [slice(gi * hrows, (gi + 1) * hrows) for gi in (2 * pair, 2 * pair + 1)]
        zs = [v_ref[0, rs, :] for rs in rss]
        for o, gate_ref in enumerate((x1_ref, x2_ref)):
            zps = []
            for x, hs in zip(_cm_forward(zs[0], zs[1], lhs_r, tw_r, tw_i, rm), hss):
                xr, xi = x[:, :HY_LANE], x[:, HY_LANE:]
                hr, hi = h_ref[o, hs, :HY_LANE], h_ref[o, hs, HY_LANE:]
                y = jnp.concatenate([xr * hr - xi * hi, xr * hi + xi * hr], axis=1).astype(BF16)
                zc = jnp.dot(y, rm_inv, preferred_element_type=F32)
                zr, zi = zc[:, :HY_LANE], zc[:, HY_LANE:]
                zps.append(jnp.concatenate([zr * tw_r + zi * tw_i, zi * tw_r - zr * tw_i], axis=0).astype(BF16))
            conv = jnp.dot(lhs_rinv, jnp.concatenate(zps, axis=1), preferred_element_type=F32)
            zs = [gate_ref[0, rs, :] * (conv[:, i * HY_LANE:(i + 1) * HY_LANE] + z * bias_ref[o, rs, :])
                  for i, (z, rs) in enumerate(zip(zs, rss))]
        for z, rs in zip(zs, rss):
            o_ref[0, rs, :] = z


def _cm_spectrum_kernel(f_ref, b_ref, lr_ref, rm_ref, tw_ref, o_ref):
    xf, xb = _cm_forward(f_ref[...], b_ref[...], lr_ref[...], tw_ref[0], tw_ref[1], rm_ref[...])
    o_ref[0, :, :HY_LANE] = xf[:, :HY_LANE] + xb[:, :HY_LANE]
    o_ref[0, :, HY_LANE:] = xf[:, HY_LANE:] - xb[:, HY_LANE:]


def _to_channel_major(t):
    B, L, C = t.shape
    return jnp.transpose(t, (0, 2, 1)).reshape(B, C * (L // HY_LANE), HY_LANE)


def _hyena(dp, sconv_w, sconv_b, w1, b1, w2, b2, w3, fbias):
    B, L, _ = dp.shape
    W = HY_WIDTH
    cst = _cm_consts(L)
    mh, m2n, g = cst["mh"], cst["m2n"], cst["g"]
    lhs_r, lhs_rinv = jnp.asarray(cst["lhs_r"], BF16), jnp.asarray(cst["lhs_rinv"], BF16)
    rm, rm_inv = jnp.asarray(cst["rm"], BF16), jnp.asarray(cst["rm_inv"], BF16)
    tw = jnp.asarray(cst["tw"])
    rows, hrows = HY_GROUP_ROWS, g * m2n

    filt = _to_channel_major(_hy_filters(L, w1, b1, w2, b2, w3))[0]
    ng = W // g
    spec = pl.pallas_call(
        _cm_spectrum_kernel,
        grid=(HY_ORDER, ng),
        in_specs=[pl.BlockSpec((rows, HY_LANE), lambda o, j: ((2 * o) * ng + j, 0)),
                  pl.BlockSpec((rows, HY_LANE), lambda o, j: ((2 * o + 1) * ng + j, 0)),
                  _const_spec(lhs_r.shape), _const_spec(rm.shape), _const_spec(tw.shape)],
        out_specs=pl.BlockSpec((1, hrows, 2 * HY_LANE), lambda o, j: (o, j, 0)),
        out_shape=jax.ShapeDtypeStruct((HY_ORDER, W * m2n, 2 * HY_LANE), F32),
        compiler_params=_cparams(2),
        name="hy_spectrum",
    )(filt, filt, lhs_r, rm, tw)

    x1, x2, v = _sconv(dp, sconv_w, sconv_b)
    x1, x2, v = _to_channel_major(x1), _to_channel_major(x2), _to_channel_major(v)
    bias = jnp.broadcast_to(jnp.repeat(fbias, mh, axis=1)[:, :, None], (HY_ORDER, W * mh, HY_LANE))
    groups = min(4, ng)
    assert groups % 2 == 0 and ng % groups == 0
    seq =pl.BlockSpec((1, groups * rows, HY_LANE), lambda j, b: (b, j, 0))
    out = pl.pallas_call(
        functools.partial(_cm_conv_kernel, groups=groups),
        grid=(ng // groups, B),
        in_specs=[seq, seq, seq,
                  pl.BlockSpec((HY_ORDER, groups * rows, HY_LANE), lambda j, b: (0, j, 0)),
                  pl.BlockSpec((HY_ORDER, groups * hrows, 2 * HY_LANE), lambda j, b: (0, j, 0)),
                  _const_spec(lhs_r.shape), _const_spec(lhs_rinv.shape), _const_spec(rm.shape),
                  _const_spec(rm_inv.shape), _const_spec(tw.shape)],
        out_specs=seq,
        out_shape=jax.ShapeDtypeStruct((B, W * mh, HY_LANE), F32),
        compiler_params=_cparams(2),
        name="hy_conv",
    )(v, x1, x2, bias, spec, lhs_r, lhs_rinv, rm, rm_inv, tw)
    return jnp.transpose(out.reshape(B, W, L), (0, 2, 1))


def _swap_perm(width):
    q = width // 4
    return np.concatenate([np.arange(q, 2 * q), np.arange(0, q), np.arange(3 * q, 4 * q), np.arange(2 * q, 3 * q)])


def _rope_tables(L, width, latent):
    if not latent:
        return jnp.ones((L, width), F32), jnp.zeros((L, width), F32)
    q = width // 4
    t = jnp.arange(L, dtype=jnp.int32)
    row = (t // GRID_W).astype(F32)
    col = (t % GRID_W).astype(F32)
    freqs = ROPE_THETA ** (-jnp.arange(q, dtype=F32) / q)
    ar, ac = row[:, None] * freqs[None, :], col[:, None] * freqs[None, :]
    cos = jnp.concatenate([jnp.cos(ar), jnp.cos(ar), jnp.cos(ac), jnp.cos(ac)], axis=-1)
    sin = jnp.concatenate([-jnp.sin(ar), jnp.sin(ar), -jnp.sin(ac), jnp.sin(ac)], axis=-1)
    return cos, sin


def _tables(L, latent):
    ca, sa = _rope_tables(L, MLA_ROPE, latent)
    pad = lambda t, fill: jnp.concatenate(
        [jnp.full((L, MLA_NOPE), fill, F32), t, jnp.zeros((L, MLA_DPAD - MLA_NOPE - MLA_ROPE), F32)], axis=-1)
    cc, sc = _rope_tables(L, GQA_HEAD_DIM, latent)
    return pad(ca, 1.0), pad(sa, 0.0), cc, sc


def _prep_layer_weights(w_in, w_uq, w_ukv, gmq, gmkv, gla_w_gate, gla_b_gate, gcq, gck):
    D = w_in.shape[0]
    col = lambda i: w_in[:, IN_OFFS[i]:IN_OFFS[i + 1]]
    a_cq, a_ckv, a_kr, b_q, b_k, b_v, b_lr, b_og, c_q, c_k, c_v, d_p, gates = [col(i) for i in range(13)]
    p32, p64 = _swap_perm(MLA_ROPE), _swap_perm(GQA_HEAD_DIM)
    z = lambda n: jnp.zeros((D, n), F32)
    pad_kr = lambda t: jnp.concatenate([z(MLA_NOPE), t, z(MLA_DPAD - MLA_NOPE - MLA_ROPE)], axis=1)
    swap_heads = lambda t, n: t.reshape(D, n, GQA_HEAD_DIM)[:, :, p64].reshape(D, n * GQA_HEAD_DIM)
    w_att = jnp.concatenate([a_cq, a_ckv, pad_kr(a_kr), pad_kr(a_kr[:, p32]),
                             c_q, swap_heads(c_q, GQA_HEADS), c_k, swap_heads(c_k, GQA_KV_HEADS), c_v], axis=1)
    w_mix = jnp.concatenate([b_q, b_k, b_v, b_og, b_lr, z(128 - 2 * GLA_GATE_RANK), d_p, gates], axis=1)

    uq = w_uq.reshape(MLA_Q_LORA, MLA_HEADS, MLA_NOPE + MLA_ROPE)
    zq = jnp.zeros((MLA_Q_LORA, MLA_HEADS, MLA_DPAD - MLA_NOPE - MLA_ROPE), F32)
    uq_pad = jnp.concatenate([uq, zq], axis=-1).reshape(MLA_Q_LORA, MLA_HEADS * MLA_DPAD)
    uq_sw = jnp.concatenate([uq[..., :MLA_NOPE], uq[..., MLA_NOPE:][..., p32], zq], axis=-1)
    wuq2 = jnp.concatenate([uq_pad, uq_sw.reshape(MLA_Q_LORA, MLA_HEADS * MLA_DPAD)], axis=1)
    ukv = w_ukv.reshape(MLA_KV_LORA, MLA_HEADS, MLA_NOPE + MLA_V)
    zk = jnp.zeros((MLA_KV_LORA, MLA_HEADS, MLA_DPAD - MLA_NOPE), F32)
    uk = jnp.concatenate([ukv[..., :MLA_NOPE], zk], axis=-1).reshape(MLA_KV_LORA, MLA_HEADS * MLA_DPAD)
    uv = ukv[..., MLA_NOPE:].reshape(MLA_KV_LORA, MLA_HEADS * MLA_V)
    wukv2 = jnp.concatenate([uk, uv], axis=1)

    hk = GLA_HEADS * GLA_DK
    wgate = jnp.zeros((128, 2 * hk), F32)
    wgate = wgate.at[:GLA_GATE_RANK, :hk].set(gla_w_gate[0]).at[GLA_GATE_RANK:2 * GLA_GATE_RANK, hk:].set(gla_w_gate[1])
    return dict(w_att=w_att.astype(BF16), w_mix=w_mix.astype(BF16), wuq2=wuq2.astype(BF16), wukv2=wukv2.astype(BF16),
                gmq=gmq.reshape(1, -1), gmkv=gmkv.reshape(1, -1),
                gcq=jnp.stack([gcq, gcq[p64]]), gck=jnp.stack([gck, gck[p64]]),
                wgate=wgate.astype(BF16), bgate=gla_b_gate.reshape(1, 2 * hk))


def kernel(x, c, ctx, c_ctx, ada_w, ada_b, norm_g, ffn_w_gate, ffn_w_up, ffn_w_down, w_in, mla_q_norm_g, mla_w_uq, mla_kv_norm_g, mla_w_ukv, gla_w_gate, gla_b_gate, gla_norm_g, gqa_q_norm_g, gqa_k_norm_g, hy_sconv_w, hy_sconv_b, hy_filt_w1, hy_filt_b1, hy_filt_w2, hy_filt_b2, hy_filt_w3, hy_filt_bias, w_branch, w_out, final_g):
    B, L, D = x.shape
    Lc = ctx.shape[1]
    depth = ada_w.shape[0]
    tabs_lat = _tables(L, True)
    tabs_ctx = _tables(Lc, False)
    cvec = jnp.concatenate([c, c_ctx[None]], axis=0)
    xc = ctx
    for l in range(depth):
        need_ctx = l < depth - 1
        mod = _ada(cvec, ada_w[l], ada_b[l]).reshape(B + 1, N_ADA, D)
        mod_lat = mod[:B]
        mod_ctx = jnp.broadcast_to(mod[B:], (B, N_ADA, D))
        wg, wu, wd = ffn_w_gate[l].astype(BF16), ffn_w_up[l].astype(BF16), ffn_w_down[l].astype(BF16)
        wp = _prep_layer_weights(w_in[l], mla_w_uq[l], mla_w_ukv[l], mla_q_norm_g[l], mla_kv_norm_g[l],
                                 gla_w_gate[l], gla_b_gate[l], gqa_q_norm_g[l], gqa_k_norm_g[l])
        wb, wo = w_branch[l].astype(BF16), w_out[l].astype(BF16)

        x = _ffn(x, mod_lat[:, 0:3], norm_g[l, 0], wg[0], wu[0], wd[0])
        xc = _ffn(xc, mod_ctx[:, 0:3], norm_g[l, 0], wg[0], wu[0], wd[0])

        qa, ka, va, qc, kc, vc = _attn_prep(x, mod_lat[:, 3:6], norm_g[l, 1], wp, tabs_lat)
        qa_c, ka_c, va_c, qc_c, kc_c, vc_c = _attn_prep(xc, mod_ctx[:, 3:6], norm_g[l, 1], wp, tabs_ctx)
        gq, gk, gv, la, og, dp, gates = _mix_prep(x, mod_lat[:, 3:6], norm_g[l, 1], wp)
        gq_c, gk_c, gv_c, la_c, og_c, dp_c, gates_c = _mix_prep(xc, mod_ctx[:, 3:6], norm_g[l, 1], wp)

        cat = lambda a, b: jnp.concatenate([a, b], axis=2)
        o_a = _attend(qa, cat(ka, ka_c), cat(va, va_c))
        qc_g = qc.reshape(B, GQA_KV_HEADS, (GQA_HEADS // GQA_KV_HEADS) * L, GQA_HEAD_DIM)
        o_c = _attend(qc_g, cat(kc, kc_c), cat(vc, vc_c)).reshape(B, GQA_HEADS, L, GQA_HEAD_DIM)

        s0 = jnp.zeros((B, GLA_HEADS, GLA_DV, GLA_DK), F32)
        ob_c, s_f, s_b = _gla(gq_c, gk_c, gv_c, la_c, s0, s0)
        o_b, _, _ = _gla(gq, gk, gv, la, s_f, s_b)

        hy = (hy_sconv_w[l], hy_sconv_b[l], hy_filt_w1[l], hy_filt_b1[l], hy_filt_w2[l], hy_filt_b2[l],
              hy_filt_w3[l], hy_filt_bias[l])
        o_d = _hyena(dp, *hy)

        x = _merge(x, mod_lat[:, 5:6], o_a, o_b, og, gla_norm_g[l], o_c, o_d, gates, wb, wo)
        last = l == depth - 1
        x = _ffn(x, mod_lat[:, 6:9], norm_g[l, 2], wg[1], wu[1], wd[1], final_g=final_g if last else None)
        if need_ctx:
            o_ac = _attend(qa_c, ka_c, va_c)
            qcc_g = qc_c.reshape(B, GQA_KV_HEADS, (GQA_HEADS // GQA_KV_HEADS) * Lc, GQA_HEAD_DIM)
            o_cc = _attend(qcc_g, kc_c, vc_c).reshape(B, GQA_HEADS, Lc, GQA_HEAD_DIM)
            o_dc = _hyena(dp_c, *hy)
            xc = _merge(xc, mod_ctx[:, 5:6], o_ac, ob_c, og_c, gla_norm_g[l], o_cc, o_dc, gates_c, wb, wo)
            xc = _ffn(xc, mod_ctx[:, 6:9], norm_g[l, 2], wg[1], wu[1], wd[1])
    return x
```

```python
import functools
import math

import jax
import jax.numpy as jnp
import numpy as np
from jax import lax
from jax.experimental import pallas as pl
from jax.experimental.pallas import tpu as pltpu

F32 = jnp.float32
BF16 = jnp.bfloat16

D_MODEL = 1024
GRID_W = 64
N_ADA = 9
FFN_DIM = 2816
RMS_EPS = 1e-6
ROPE_THETA = 10000.0
MLA_HEADS = 8
MLA_NOPE = 64
MLA_ROPE = 32
MLA_V = 64
MLA_Q_LORA = 256
MLA_KV_LORA = 128
MLA_SCALE = (MLA_NOPE + MLA_ROPE) ** -0.5
MLA_DPAD = 128
GLA_HEADS = 4
GLA_DK = 64
GLA_DV = 128
GLA_GATE_RANK = 16
GLA_GATE_TEMP = 16.0
GLA_CHUNK = 64
GQA_HEADS = 8
GQA_KV_HEADS = 2
GQA_HEAD_DIM = 64
GQA_SCALE = GQA_HEAD_DIM ** -0.5
HY_WIDTH = 512
HY_ORDER = 2
HY_BANDS = 16
HY_SIN_FREQ = 1.0
HY_DECAY_TARGET = 1e-2
HY_SHORT_DECAY_PCT = 0.3
HY_LONG_DECAY_PCT = 1.5
N_BRANCH = 4
BRANCH_WIDTH = 512
LOG2E = math.log2(math.e)

IN_SIZES = (MLA_Q_LORA, MLA_KV_LORA, MLA_ROPE,
            GLA_HEADS * GLA_DK, GLA_HEADS * GLA_DK, GLA_HEADS * GLA_DV, 2 * GLA_GATE_RANK, GLA_HEADS * GLA_DV,
            GQA_HEADS * GQA_HEAD_DIM, GQA_KV_HEADS * GQA_HEAD_DIM, GQA_KV_HEADS * GQA_HEAD_DIM,
            (HY_ORDER + 1) * HY_WIDTH,
            N_BRANCH * D_MODEL)
IN_OFFS = tuple(int(v) for v in np.cumsum((0,) + IN_SIZES))

VMEM_LIMIT_BYTES = 56 * 1024 * 1024


def _cparams(n_axes):
    return pltpu.CompilerParams(dimension_semantics=("arbitrary",) * n_axes,
                                vmem_limit_bytes=VMEM_LIMIT_BYTES)


def _const_spec(shape):
    nd = len(shape)
    return pl.BlockSpec(shape, lambda *_: (0,) * nd, pipeline_mode=pl.Buffered(1))


def _pick_tile(n, pref):
    t = min(n, pref)
    while n % t:
        t //= 2
    return t


def _rms(x):
    return x * lax.rsqrt(jnp.mean(x * x, axis=-1, keepdims=True) + RMS_EPS)


def _norm_mod(x, g, shift, scale):
    return (_rms(x) * g) * (1.0 + scale) + shift


def _ada_kernel(s_ref, w_ref, b_ref, o_ref):
    s = s_ref[...]
    s = s * jax.nn.sigmoid(s)
    o_ref[...] = jnp.dot(s.astype(BF16), w_ref[...].astype(BF16), preferred_element_type=F32) + b_ref[...]


def _ada(cvec, w, b):
    rows, d = cvec.shape
    n = w.shape[1]
    tn = _pick_tile(n, 1152)
    return pl.pallas_call(
        _ada_kernel,
        grid=(n // tn,),
        in_specs=[pl.BlockSpec((rows, d), lambda j: (0, 0)),
                  pl.BlockSpec((d, tn), lambda j: (0, j)),
                  pl.BlockSpec((1, tn), lambda j: (0, j))],
        out_specs=pl.BlockSpec((rows, tn), lambda j: (0, j)),
        out_shape=jax.ShapeDtypeStruct((rows, n), F32),
        compiler_params=_cparams(1),
        name="ada_mod",
    )(cvec, w, b.reshape(1, n))


def _ffn_kernel(x_ref, mod_ref, g_ref, wg_ref, wu_ref, wd_ref, fg_ref, o_ref, *, final_norm):
    x = x_ref[0]
    mod = mod_ref[0]
    h = _norm_mod(x, g_ref[...], mod[0:1], mod[1:2]).astype(BF16)
    a = jnp.dot(h, wg_ref[...], preferred_element_type=F32)
    b = jnp.dot(h, wu_ref[...], preferred_element_type=F32)
    act = (a * jax.nn.sigmoid(a) * b).astype(BF16)
    y = jnp.dot(act, wd_ref[...], preferred_element_type=F32)
    out = x + (0.5 * mod[2:3]) * y
    if final_norm:
        out = _rms(out) * fg_ref[...]
    o_ref[0] = out


def _ffn(x, mod3, g, wg, wu, wd, final_g=None):
    B, L, D = x.shape
    F = wg.shape[1]
    tm = _pick_tile(L, 512)
    final_norm = final_g is not None
    fg = (final_g if final_norm else jnp.ones((D,), F32)).reshape(1, D)
    return pl.pallas_call(
        functools.partial(_ffn_kernel, final_norm=final_norm),
        grid=(B, L // tm),
        in_specs=[pl.BlockSpec((1, tm, D), lambda b, i: (b, i, 0)),
                  pl.BlockSpec((1, 3, D), lambda b, i: (b, 0, 0)),
                  _const_spec((1, D)), _const_spec((D, F)), _const_spec((D, F)), _const_spec((F, D)),
                  _const_spec((1, D))],
        out_specs=pl.BlockSpec((1, tm, D), lambda b, i: (b, i, 0)),
        out_shape=jax.ShapeDtypeStruct((B, L, D), F32),
        compiler_params=_cparams(2),
        name="ffn",
    )(x, mod3, g.reshape(1, D), wg, wu, wd, fg)


QA_SCALE = MLA_SCALE * LOG2E
QC_SCALE = GQA_SCALE * LOG2E
_A_CQ, _A_CKV, _A_KR, _A_KRS, _C_Q, _C_QS, _C_K, _C_KS, _C_V, _ATT_COLS = (
    0, 256, 384, 512, 640, 1152, 1664, 1792, 1920, 2048)


def _attn_prep_kernel(x_ref, mod_ref, g_ref, w_ref, wuq_ref, wukv_ref, gmq_ref, gmkv_ref, gcq_ref, gck_ref, ones_ref,
                      cosa_ref, sina_ref, cosc_ref, sinc_ref,
                      qa_ref, ka_ref, va_ref, qc_ref, kc_ref, vc_ref):
    x = x_ref[0]
    mod = mod_ref[0]
    h = _norm_mod(x, g_ref[...], mod[0:1], mod[1:2]).astype(BF16)
    p = jnp.dot(h, w_ref[...], preferred_element_type=F32)
    cosa, sina = cosa_ref[...], sina_ref[...]
    cosc, sinc = cosc_ref[...], sinc_ref[...]

    kr = p[:, _A_KR:_A_KR + 128] * cosa + p[:, _A_KRS:_A_KRS + 128] * sina
    nq = (_rms(p[:, _A_CQ:_A_CQ + MLA_Q_LORA]) * gmq_ref[...]).astype(BF16)
    q2 = jnp.dot(nq, wuq_ref[...], preferred_element_type=F32)
    nkv = (_rms(p[:, _A_CKV:_A_CKV + MLA_KV_LORA]) * gmkv_ref[...]).astype(BF16)
    kv = jnp.dot(nkv, wukv_ref[...], preferred_element_type=F32)
    hq = MLA_HEADS * MLA_DPAD
    for hd in range(MLA_HEADS):
        lo = hd * MLA_DPAD
        qh = q2[:, lo:lo + MLA_DPAD] * cosa + q2[:, hq + lo:hq + lo + MLA_DPAD] * sina
        qa_ref[0, hd] = (qh * QA_SCALE).astype(BF16)
        ka_ref[0, hd] = (kv[:, lo:lo + MLA_DPAD] + kr).astype(BF16)
        va_ref[0, hd] = kv[:, hq + hd * MLA_V:hq + (hd + 1) * MLA_V].astype(BF16)

    def normed_rot(t, ts, gains, ones_bd, scale):
        sq = t * t
        hi = sq.astype(BF16)
        lo = (sq - hi.astype(F32)).astype(BF16)
        ssum = jnp.dot(hi, ones_bd, preferred_element_type=F32) + jnp.dot(lo, ones_bd, preferred_element_type=F32)
        inv = lax.rsqrt(ssum * (1.0 / GQA_HEAD_DIM) + RMS_EPS)
        reps = t.shape[1] // ATT_LANE
        gc = jnp.concatenate([(gains[0:1] * scale) * cosc] * reps, axis=1)
        gs = jnp.concatenate([(gains[1:2] * scale) * sinc] * reps, axis=1)
        return inv * (t * gc + ts * gs)

    dh = GQA_HEAD_DIM
    wq, wk = GQA_HEADS * dh, GQA_KV_HEADS * dh
    q_all = normed_rot(p[:, _C_Q:_C_Q + wq], p[:, _C_QS:_C_QS + wq], gcq_ref[...], ones_ref[...], QC_SCALE)
    k_all = normed_rot(p[:, _C_K:_C_K + wk], p[:, _C_KS:_C_KS + wk], gck_ref[...], ones_ref[:wk, :wk], 1.0)
    for hd in range(GQA_HEADS):
        qc_ref[0, hd] = q_all[:, hd * dh:(hd + 1) * dh].astype(BF16)
    for hd in range(GQA_KV_HEADS):
        kc_ref[0, hd] = k_all[:, hd * dh:(hd + 1) * dh].astype(BF16)
        vc_ref[0, hd] = p[:, _C_V + hd * dh:_C_V + (hd + 1) * dh].astype(BF16)


def _attn_prep(x, mod3, g, wp, tabs):
    B, L, D = x.shape
    tm = _pick_tile(L, 512)
    cosa, sina, cosc, sinc = tabs
    tok = lambda w: pl.BlockSpec((1, tm, w), lambda b, i: (b, i, 0))
    heads = lambda n, w: pl.BlockSpec((1, n, tm, w), lambda b, i: (b, 0, i, 0))
    tab = lambda w: pl.BlockSpec((tm, w), lambda b, i: (i, 0))
    out_shape = (jax.ShapeDtypeStruct((B, MLA_HEADS, L, MLA_DPAD), BF16),
                 jax.ShapeDtypeStruct((B, MLA_HEADS, L, MLA_DPAD), BF16),
                 jax.ShapeDtypeStruct((B, MLA_HEADS, L, MLA_V), BF16),
                 jax.ShapeDtypeStruct((B, GQA_HEADS, L, GQA_HEAD_DIM), BF16),
                 jax.ShapeDtypeStruct((B, GQA_KV_HEADS, L, GQA_HEAD_DIM), BF16),
                 jax.ShapeDtypeStruct((B, GQA_KV_HEADS, L, GQA_HEAD_DIM), BF16))
    return pl.pallas_call(
        _attn_prep_kernel,
        grid=(B, L // tm),
        in_specs=[tok(D), pl.BlockSpec((1, 3, D), lambda b, i: (b, 0, 0)), _const_spec((1, D)),
                  _const_spec(wp["w_att"].shape), _const_spec(wp["wuq2"].shape), _const_spec(wp["wukv2"].shape),
                  _const_spec((1, MLA_Q_LORA)), _const_spec((1, MLA_KV_LORA)),
                  _const_spec((2, ATT_LANE)), _const_spec((2, ATT_LANE)), _const_spec(wp["ones_bd"].shape),
                  tab(MLA_DPAD), tab(MLA_DPAD), tab(ATT_LANE), tab(ATT_LANE)],
        out_specs=(heads(MLA_HEADS, MLA_DPAD), heads(MLA_HEADS, MLA_DPAD), heads(MLA_HEADS, MLA_V),
                   heads(GQA_HEADS, GQA_HEAD_DIM), heads(GQA_KV_HEADS, GQA_HEAD_DIM),
                   heads(GQA_KV_HEADS, GQA_HEAD_DIM)),
        out_shape=out_shape,
        compiler_params=_cparams(2),
        name="attn_prep",
    )(x, mod3, g.reshape(1, D), wp["w_att"], wp["wuq2"], wp["wukv2"], wp["gmq"], wp["gmkv"], wp["gcq"], wp["gck"],
      wp["ones_bd"], cosa, sina, cosc, sinc)


_B_Q, _B_K, _B_V, _B_OG, _B_LR, _D_P, _GATES, _MIX_COLS = (0, 256, 512, 1024, 1536, 1664, 3200, 7296)


def _mix_prep_kernel(x_ref, mod_ref, g_ref, w_ref, wgate_ref, bgate_ref,
                     gq_ref, gk_ref, gv_ref, la_ref, og_ref, dp_ref, gates_ref):
    x = x_ref[0]
    mod = mod_ref[0]
    h = _norm_mod(x, g_ref[...], mod[0:1], mod[1:2]).astype(BF16)
    pg = jnp.dot(h, w_ref[:, _B_Q:_D_P], preferred_element_type=F32)
    gq_ref[0] = pg[:, _B_Q:_B_K] * (GLA_DK ** -0.5)
    gk_ref[0] = pg[:, _B_K:_B_V]
    gv_ref[0] = pg[:, _B_V:_B_OG]
    og_ref[0] = pg[:, _B_OG:_B_LR]
    pre = jnp.dot(pg[:, _B_LR:_D_P].astype(BF16), wgate_ref[...], preferred_element_type=F32) + bgate_ref[...]
    la = (jnp.minimum(pre, 0.0) - jnp.log1p(jnp.exp(-jnp.abs(pre)))) * (1.0 / GLA_GATE_TEMP)
    C = GLA_CHUNK
    ii = lax.broadcasted_iota(jnp.int32, (C, C), 0)
    jj = lax.broadcasted_iota(jnp.int32, (C, C), 1)
    tri_low = jnp.where(ii >= jj, 1.0, 0.0).astype(BF16)
    tri_up = jnp.where(ii <= jj, 1.0, 0.0).astype(BF16)
    hi = la.astype(BF16)
    lo = (la - hi.astype(F32)).astype(BF16)
    hk = GLA_HEADS * GLA_DK
    for c in range(la.shape[0] // C):
        rs = slice(c * C, (c + 1) * C)
        la_ref[0, rs, :hk] = (jnp.dot(tri_low, hi[rs, :hk], preferred_element_type=F32)
                              + jnp.dot(tri_low, lo[rs, :hk], preferred_element_type=F32))
        la_ref[0, rs, hk:] = (jnp.dot(tri_up, hi[rs, hk:], preferred_element_type=F32)
                              + jnp.dot(tri_up, lo[rs, hk:], preferred_element_type=F32))
    dp_ref[0] = jnp.dot(h, w_ref[:, _D_P:_GATES], preferred_element_type=F32)
    gates_ref[0] = jax.nn.sigmoid(jnp.dot(h, w_ref[:, _GATES:_MIX_COLS], preferred_element_type=F32)).astype(BF16)


def _mix_prep(x, mod3, g, wp):
    B, L, D = x.shape
    tm = _pick_tile(L, 256)
    tok = lambda w: pl.BlockSpec((1, tm, w), lambda b, i: (b, i, 0))
    widths = (256, 256, 512, 512, 512, 1536, 4096)
    dtypes = (F32, F32, F32, F32, F32, F32, BF16)
    return pl.pallas_call(
        _mix_prep_kernel,
        grid=(B, L // tm),
        in_specs=[tok(D), pl.BlockSpec((1, 3, D), lambda b, i: (b, 0, 0)), _const_spec((1, D)),
                  _const_spec(wp["w_mix"].shape), _const_spec(wp["wgate"].shape), _const_spec((1, 512))],
        out_specs=tuple(tok(w) for w in widths),
        out_shape=tuple(jax.ShapeDtypeStruct((B, L, w), dt) for w, dt in zip(widths, dtypes)),
        compiler_params=_cparams(2),
        name="mix_prep",
    )(x, mod3, g.reshape(1, D), wp["w_mix"], wp["wgate"], wp["bgate"])


ATT_LANE = 128
ATT_TK = 256


def _lane_fold(x, op):
    acc = x[:, :ATT_LANE]
    for j in range(1, x.shape[1] // ATT_LANE):
        acc = op(acc, x[:, j * ATT_LANE:(j + 1) * ATT_LANE])
    return acc


def _attn_kernel(q_ref, k_ref, v_ref, o_ref, *, tk):
    q = q_ref[0, 0]
    nck = k_ref.shape[2] // tk
    nt = (((1,), (1,)), ((), ()))
    m = ls = acc = None
    for c in range(nck):
        s = lax.dot_general(q, k_ref[0, 0, c * tk:(c + 1) * tk, :], nt, preferred_element_type=F32)
        cm = jnp.max(s, axis=-1, keepdims=True)
        m_new = cm if c == 0 else jnp.maximum(m, cm)
        p = jnp.exp2(s - m_new)
        cl = _lane_fold(p, jnp.add)
        pv = jnp.dot(p.astype(BF16), v_ref[0, 0, c * tk:(c + 1) * tk, :], preferred_element_type=F32)
        if c == 0:
            ls, acc = cl, pv
        else:
            alpha = jnp.exp2(m - m_new)
            ls = alpha * ls + cl
            acc = alpha * acc + pv
        m = m_new
    l = jnp.sum(ls, axis=-1, keepdims=True)
    o_ref[0, 0] = (acc / l).astype(o_ref.dtype)


def _attend(q, k, v, group=1):
    B, H, Lq, d = q.shape
    Lk, dv = v.shape[2], v.shape[3]
    tq = _pick_tile(Lq, 1024)
    tk = _pick_tile(Lk, ATT_TK)
    return pl.pallas_call(
        functools.partial(_attn_kernel, tk=tk),
        grid=(B, H, Lq // tq),
        in_specs=[pl.BlockSpec((1, 1, tq, d), lambda b, h, i: (b, h, i, 0)),
                  pl.BlockSpec((1, 1, Lk, d), lambda b, h, i: (b, h // group, 0, 0)),
                  pl.BlockSpec((1, 1, Lk, dv), lambda b, h, i: (b, h // group, 0, 0))],
        out_specs=pl.BlockSpec((1, 1, tq, dv), lambda b, h, i: (b, h, i, 0)),
        out_shape=jax.ShapeDtypeStruct((B, H, Lq, dv), BF16),
        compiler_params=_cparams(3),
        name="attend",
    )(q, k, v)


def _merge_kernel(x_ref, mod_ref, oa_ref, ob_ref, og_ref, gn_ref, oc_ref, od_ref, gates_ref, wb_ref, wo_ref, o_ref):
    oa = jnp.concatenate([oa_ref[0, hd] for hd in range(MLA_HEADS)], axis=-1)
    oc = jnp.concatenate([oc_ref[0, hd] for hd in range(GQA_HEADS)], axis=-1)
    ob = ob_ref[0]
    ob = jnp.concatenate([_rms(ob[:, hd * GLA_DV:(hd + 1) * GLA_DV]) * gn_ref[...] for hd in range(GLA_HEADS)], axis=-1)
    og = og_ref[0]
    ob = ob * (og * jax.nn.sigmoid(og))
    branches = (oa, ob.astype(BF16), oc, od_ref[0].astype(BF16))
    y = None
    for i, o in enumerate(branches):
        gate = gates_ref[0, :, i * D_MODEL:(i + 1) * D_MODEL].astype(F32)
        t = gate * jnp.dot(o, wb_ref[i], preferred_element_type=F32)
        y = t if y is None else y + t
    z = jnp.dot(y.astype(BF16), wo_ref[...], preferred_element_type=F32)
    o_ref[0] = x_ref[0] + mod_ref[0] * z


def _merge(x, g2, oa, ob, og, gla_g, oc, od, gates, wb, wo):
    B, L, D = x.shape
    tm = _pick_tile(L, 512)
    tok = lambda w: pl.BlockSpec((1, tm, w), lambda b, i: (b, i, 0))
    heads = lambda n, w: pl.BlockSpec((1, n, tm, w), lambda b, i: (b, 0, i, 0))
    return pl.pallas_call(
        _merge_kernel,
        grid=(B, L // tm),
        in_specs=[tok(D), pl.BlockSpec((1, 1, D), lambda b, i: (b, 0, 0)),
                  heads(MLA_HEADS, MLA_V), tok(BRANCH_WIDTH), tok(BRANCH_WIDTH), _const_spec((1, GLA_DV)),
                  heads(GQA_HEADS, GQA_HEAD_DIM), tok(BRANCH_WIDTH),
                  tok(N_BRANCH * D), _const_spec(wb.shape), _const_spec(wo.shape)],
        out_specs=tok(D),
        out_shape=jax.ShapeDtypeStruct((B, L, D), F32),
        compiler_params=_cparams(2),
        name="merge",
    )(x, g2, oa, ob, og, gla_g.reshape(1, GLA_DV), oc, od, gates, wb, wo)


GLA_PAIR = 2


def _gla_kernel(q_ref, k_ref, v_ref, laf_ref, lab_ref, sf0_ref, sb0_ref, o_ref, sf_ref, sb_ref, stf, stb, *, n_chunks):
    C = GLA_CHUNK
    ii = lax.broadcasted_iota(jnp.int32, (C, C), 0)
    jj = lax.broadcasted_iota(jnp.int32, (C, C), 1)
    low, up = ii >= jj, ii <= jj
    nt = (((1,), (1,)), ((), ()))
    stf[...] = sf0_ref[0]
    stb[...] = sb0_ref[0]

    def one_direction(n, mask, bc_ref, st, last_row, accumulate):
        rows = pl.ds(pl.multiple_of(n * C, C), C)
        q, k, v, bc = q_ref[0, rows, :], k_ref[0, rows, :], v_ref[0, rows, :], bc_ref[0, rows, :]
        b_last = bc[last_row:last_row + 1, :]
        q_dec = (q * jnp.exp(bc)).astype(BF16)
        k_inv = (k * jnp.exp(-bc)).astype(BF16)
        k_end = (k * jnp.exp(b_last - bc)).astype(BF16)
        decay = jnp.exp(b_last)
        outs = []
        for h in range(GLA_PAIR):
            ks = slice(h * GLA_DK, (h + 1) * GLA_DK)
            vh = v[:, h * GLA_DV:(h + 1) * GLA_DV]
            a = lax.dot_general(q_dec[:, ks], k_inv[:, ks], nt, preferred_element_type=F32)
            a = jnp.where(mask, a, 0.0).astype(BF16)
            s_t = st[h]
            outs.append(jnp.dot(a, vh.astype(BF16), preferred_element_type=F32)
                        + lax.dot_general(q_dec[:, ks], s_t.astype(BF16), nt, preferred_element_type=F32))
            st[h] = s_t * decay[:, ks] + jnp.dot(vh.T.astype(BF16), k_end[:, ks], preferred_element_type=F32)
        o = jnp.concatenate(outs, axis=-1)
        if accumulate:
            o_ref[0, rows, :] = o_ref[0, rows, :] + o
        else:
            o_ref[0, rows, :] = o

    def sweep(accumulate):
        def body(n, carry):
            one_direction(n, low, laf_ref, stf, C - 1, accumulate)
            one_direction(n_chunks - 1 - n, up, lab_ref, stb, 0, accumulate)
            return carry
        return body

    half = n_chunks // 2
    lax.fori_loop(0, half, sweep(False), 0, unroll=2)
    lax.fori_loop(half, n_chunks, sweep(True), 0, unroll=2)
    sf_ref[0] = stf[...]
    sb_ref[0] = stb[...]


def _gla(q, k, v, la, sf0, sb0):
    B, L, _ = q.shape
    n_chunks = L // GLA_CHUNK
    assert L % GLA_CHUNK == 0 and n_chunks % 2 == 0
    wk, wv, npair = GLA_PAIR * GLA_DK, GLA_PAIR * GLA_DV, GLA_HEADS // GLA_PAIR
    seq = lambda w, off: pl.BlockSpec((1, L, w), lambda b, p: (b, 0, p + off))
    st_spec = pl.BlockSpec((1, GLA_PAIR, GLA_DV, GLA_DK), lambda b, p: (b, p, 0, 0))
    st_shape = jax.ShapeDtypeStruct((B, GLA_HEADS, GLA_DV, GLA_DK), F32)
    return pl.pallas_call(
        functools.partial(_gla_kernel, n_chunks=n_chunks),
        grid=(B, npair),
        in_specs=[seq(wk, 0), seq(wk, 0), seq(wv, 0), seq(wk, 0), seq(wk, npair), st_spec, st_spec],
        out_specs=(seq(wv, 0), st_spec, st_spec),
        out_shape=(jax.ShapeDtypeStruct((B, L, GLA_HEADS * GLA_DV), F32), st_shape, st_shape),
        scratch_shapes=[pltpu.VMEM((GLA_PAIR, GLA_DV, GLA_DK), F32), pltpu.VMEM((GLA_PAIR, GLA_DV, GLA_DK), F32)],
        compiler_params=_cparams(2),
        name="gla",
    )(q, k, v, la, la, sf0, sb0)


HY_LANE = 128


def _sconv_kernel(p1_ref, p2_ref, p3_ref, w1_ref, w2_ref, w3_ref, b1_ref, b2_ref, b3_ref, x1_ref, x2_ref, v_ref):
    L = p1_ref.shape[1]
    row = lax.broadcasted_iota(jnp.int32, (L, HY_LANE), 0)
    for p_ref, w_ref, b_ref, o_ref in ((p1_ref, w1_ref, b1_ref, x1_ref), (p2_ref, w2_ref, b2_ref, x2_ref),
                                       (p3_ref, w3_ref, b3_ref, v_ref)):
        x = p_ref[0]
        prev = jnp.where(row == 0, 0.0, pltpu.roll(x, 1, 0))
        nxt = jnp.where(row == L - 1, 0.0, pltpu.roll(x, L - 1, 0))
        o_ref[0] = prev * w_ref[0:1, :] + x * w_ref[1:2, :] + nxt * w_ref[2:3, :] + b_ref[...]


def _sconv(p, w, b):
    B, L, _ = p.shape
    nb = HY_WIDTH // HY_LANE
    pspec = lambda g: pl.BlockSpec((1, L, HY_LANE), lambda bb, j: (bb, 0, g * nb + j))
    wspec = lambda g: pl.BlockSpec((3, HY_LANE), lambda bb, j: (0, g * nb + j))
    bspec = lambda g: pl.BlockSpec((1, HY_LANE), lambda bb, j: (0, g * nb + j))
    ospec = pl.BlockSpec((1, L, HY_LANE), lambda bb, j: (bb, 0, j))
    oshape = jax.ShapeDtypeStruct((B, L, HY_WIDTH), F32)
    b2 = b.reshape(1, -1)
    return pl.pallas_call(
        _sconv_kernel,
        grid=(B, nb),
        in_specs=[pspec(0), pspec(1), pspec(2), wspec(0), wspec(1), wspec(2), bspec(0), bspec(1), bspec(2)],
        out_specs=(ospec, ospec, ospec),
        out_shape=(oshape, oshape, oshape),
        compiler_params=_cparams(2),
        name="hy_sconv",
    )(p, p, p, w, w, w, b2, b2, b2)


def _hy_filter_kernel(feat_ref, w1_ref, b1_ref, w2_ref, b2_ref, w3_ref, delta_ref, o_ref):
    hp = lax.Precision.HIGHEST
    feat = feat_ref[...]
    h = jnp.sin(HY_SIN_FREQ * (jnp.dot(feat, w1_ref[...], precision=hp, preferred_element_type=F32) + b1_ref[...]))
    h = jnp.sin(HY_SIN_FREQ * (jnp.dot(h, w2_ref[...], precision=hp, preferred_element_type=F32) + b2_ref[...]))
    h = jnp.dot(h, w3_ref[...], precision=hp, preferred_element_type=F32)
    win = jnp.exp(-feat[:, 0:1] * delta_ref[...])
    lag0 = (pl.program_id(0) == 0) & (lax.broadcasted_iota(jnp.int32, win.shape, 0) == 0)
    win_b = jnp.where(lag0, 0.0, win)
    for o in range(HY_ORDER):
        for d, wnd in enumerate((win, win_b)):
            lo = (o * 2 + d) * HY_WIDTH
            o_ref[0, :, lo:lo + HY_WIDTH] = h[:, lo:lo + HY_WIDTH] * wnd


def _hy_filters(L, w1, b1, w2, b2, w3):
    feat, deltas = _hy_feat_consts(L)
    emb = w1.shape[0]
    w1p = jnp.zeros((HY_LANE, w1.shape[1]), F32).at[:emb].set(w1)
    tl = _pick_tile(L, 512)
    nf = w3.shape[1]
    return pl.pallas_call(
        _hy_filter_kernel,
        grid=(L // tl,),
        in_specs=[pl.BlockSpec((tl, HY_LANE), lambda i: (i, 0)), _const_spec(w1p.shape), _const_spec((1, w1.shape[1])),
                  _const_spec(w2.shape), _const_spec((1, w2.shape[1])), _const_spec(w3.shape),
                  _const_spec((1, HY_WIDTH))],
        out_specs=pl.BlockSpec((1, tl, nf), lambda i: (0, i, 0)),
        out_shape=jax.ShapeDtypeStruct((1, L, nf), F32),
        compiler_params=_cparams(1),
        name="hy_filter",
    )(jnp.asarray(feat), w1p, b1.reshape(1, -1), w2, b2.reshape(1, -1), w3, jnp.asarray(deltas))


@functools.lru_cache(maxsize=None)
def _hy_feat_consts(L):
    t = np.linspace(0.0, 1.0, L)
    w = 2.0 * np.pi * np.arange(L) / L
    f = np.linspace(1e-4, HY_BANDS - 1, HY_BANDS)
    fw = w[:, None] * f[None, :]
    feat = np.zeros((L, HY_LANE), np.float32)
    feat[:, :2 * HY_BANDS + 1] = np.concatenate([t[:, None], np.cos(fw), -np.sin(fw)], axis=-1)
    deltas = np.abs(np.linspace(math.log(HY_DECAY_TARGET) / HY_LONG_DECAY_PCT,
                                math.log(HY_DECAY_TARGET) / HY_SHORT_DECAY_PCT, HY_WIDTH))
    return feat, deltas.reshape(1, HY_WIDTH).astype(np.float32)


HY_GROUP_ROWS = 256


@functools.lru_cache(maxsize=None)
def _cm_consts(L):
    assert L % HY_LANE == 0 and HY_GROUP_ROWS % (L // HY_LANE) == 0
    N = 2 * L
    mh = L // HY_LANE
    m2n = 2 * mh
    g = HY_GROUP_ROWS // mh
    kb, m2 = np.arange(m2n), np.arange(mh)
    c = np.exp(-2j * np.pi * np.outer(kb, m2) / m2n)
    eye = np.eye(g)
    lhs_r = np.concatenate([np.kron(eye, c.real), np.kron(eye, c.imag)], axis=0)
    m1 = np.arange(HY_LANE)
    tw = np.tile(np.exp(-2j * np.pi * np.outer(kb, m1) / N), (g, 1))
    f = np.exp(-2j * np.pi * np.outer(m1, m1) / HY_LANE)
    rm = np.block([[f.real, f.imag], [-f.imag, f.real]])
    rm_inv = np.block([[f.real, -f.imag], [f.imag, f.real]])
    gi = np.exp(2j * np.pi * np.outer(m2, kb) / m2n) / N
    lhs_rinv = np.concatenate([np.kron(eye, gi.real), -np.kron(eye, gi.imag)], axis=1)
    f32 = lambda m: np.ascontiguousarray(m, dtype=np.float32)
    return dict(mh=mh, m2n=m2n, g=g, lhs_r=f32(lhs_r), lhs_rinv=f32(lhs_rinv), rm=f32(rm), rm_inv=f32(rm_inv),
                tw=f32(np.stack([tw.real, tw.imag])))


def _cm_forward(za, zb, lhs_r, tw_r, tw_i, rm):
    half = lhs_r.shape[0] // 2
    b = jnp.dot(lhs_r, jnp.concatenate([za, zb], axis=1).astype(BF16), preferred_element_type=F32)
    out = []
    for lanes in (slice(0, HY_LANE), slice(HY_LANE, 2 * HY_LANE)):
        br, bi = b[:half, lanes], b[half:, lanes]
        bp = jnp.concatenate([br * tw_r - bi * tw_i, br * tw_i + bi * tw_r], axis=1).astype(BF16)
        out.append(jnp.dot(bp, rm, preferred_element_type=F32))
    return out


def _cm_conv_kernel(v_ref, x1_ref, x2_ref, bias_ref, h_ref, lr_ref, lri_ref, rm_ref, rmi_ref, tw_ref, o_ref, *, groups):
    rows = HY_GROUP_ROWS
    tw_r, tw_i = tw_ref[0], tw_ref[1]
    lhs_r, lhs_rinv, rm, rm_inv = lr_ref[...], lri_ref[...], rm_ref[...], rmi_ref[...]
    hrows = tw_r.shape[0]
    for pair in range(groups // 2):
        rss = [slice(gi * rows, (gi + 1) * rows) for gi in (2 * pair, 2 * pair + 1)]
        hss = [slice(gi * hrows, (gi + 1) * hrows) for gi in (2 * pair, 2 * pair + 1)]
        zs = [v_ref[0, rs, :] for rs in rss]
        for o, gate_ref in enumerate((x1_ref, x2_ref)):
            zps = []
            for x, hs in zip(_cm_forward(zs[0], zs[1], lhs_r, tw_r, tw_i, rm), hss):
                xr, xi = x[:, :HY_LANE], x[:, HY_LANE:]
                hr, hi = h_ref[o, hs, :HY_LANE], h_ref[o, hs, HY_LANE:]
                y = jnp.concatenate([xr * hr - xi * hi, xr * hi + xi * hr], axis=1).astype(BF16)
                zc = jnp.dot(y, rm_inv, preferred_element_type=F32)
                zr, zi = zc[:, :HY_LANE], zc[:, HY_LANE:]
                zps.append(jnp.concatenate([zr * tw_r + zi * tw_i, zi * tw_r - zr * tw_i], axis=0).astype(BF16))
            conv = jnp.dot(lhs_rinv, jnp.concatenate(zps, axis=1), preferred_element_type=F32)
            zs = [gate_ref[0, rs, :] * (conv[:, i * HY_LANE:(i + 1) * HY_LANE] + z * bias_ref[o, rs, :])
                  for i, (z, rs) in enumerate(zip(zs, rss))]
        for z, rs in zip(zs, rss):
            o_ref[0, rs, :] = z


def _cm_spectrum_kernel(f_ref, b_ref, lr_ref, rm_ref, tw_ref, o_ref, *, groups):
    rows, hrows = HY_GROUP_ROWS, tw_ref.shape[1]
    for gi in range(groups):
        rs = slice(gi * rows, (gi + 1) * rows)
        hs = slice(gi * hrows, (gi + 1) * hrows)
        xf, xb = _cm_forward(f_ref[rs, :], b_ref[rs, :], lr_ref[...], tw_ref[0], tw_ref[1], rm_ref[...])
        o_ref[0, hs, :HY_LANE] = xf[:, :HY_LANE] + xb[:, :HY_LANE]
        o_ref[0, hs, HY_LANE:] = xf[:, HY_LANE:] - xb[:, HY_LANE:]


def _to_channel_major(t):
    B, L, C = t.shape
    return jnp.transpose(t, (0, 2, 1)).reshape(B, C * (L // HY_LANE), HY_LANE)


def _hyena(dp, sconv_w, sconv_b, w1, b1, w2, b2, w3, fbias):
    B, L, _ = dp.shape
    W = HY_WIDTH
    cst = _cm_consts(L)
    mh, m2n, g = cst["mh"], cst["m2n"], cst["g"]
    lhs_r, lhs_rinv = jnp.asarray(cst["lhs_r"], BF16), jnp.asarray(cst["lhs_rinv"], BF16)
    rm, rm_inv = jnp.asarray(cst["rm"], BF16), jnp.asarray(cst["rm_inv"], BF16)
    tw = jnp.asarray(cst["tw"])
    rows, hrows = HY_GROUP_ROWS, g * m2n

    filt = _to_channel_major(_hy_filters(L, w1, b1, w2, b2, w3))[0]
    ng = W // g
    groups = min(4, ng)
    assert groups % 2 == 0 and ng % groups == 0
    nb = ng // groups
    spec = pl.pallas_call(
        functools.partial(_cm_spectrum_kernel, groups=groups),
        grid=(HY_ORDER, nb),
        in_specs=[pl.BlockSpec((groups * rows, HY_LANE), lambda o, j: ((2 * o) * nb + j, 0)),
                  pl.BlockSpec((groups * rows, HY_LANE), lambda o, j: ((2 * o + 1) * nb + j, 0)),
                  _const_spec(lhs_r.shape), _const_spec(rm.shape), _const_spec(tw.shape)],
        out_specs=pl.BlockSpec((1, groups * hrows, 2 * HY_LANE), lambda o, j: (o, j, 0)),
        out_shape=jax.ShapeDtypeStruct((HY_ORDER, W * m2n, 2 * HY_LANE), F32),
        compiler_params=_cparams(2),
        name="hy_spectrum",
    )(filt, filt, lhs_r, rm, tw)

    x1, x2, v = _sconv(dp, sconv_w, sconv_b)
    x1, x2, v = _to_channel_major(x1), _to_channel_major(x2), _to_channel_major(v)
    bias = jnp.broadcast_to(jnp.repeat(fbias, mh, axis=1)[:, :, None], (HY_ORDER, W * mh, HY_LANE))
    seq =pl.BlockSpec((1, groups * rows, HY_LANE), lambda j, b: (b, j, 0))
    out = pl.pallas_call(
        functools.partial(_cm_conv_kernel, groups=groups),
        grid=(ng // groups, B),
        in_specs=[seq, seq, seq,
                  pl.BlockSpec((HY_ORDER, groups * rows, HY_LANE), lambda j, b: (0, j, 0)),
                  pl.BlockSpec((HY_ORDER, groups * hrows, 2 * HY_LANE), lambda j, b: (0, j, 0)),
                  _const_spec(lhs_r.shape), _const_spec(lhs_rinv.shape), _const_spec(rm.shape),
                  _const_spec(rm_inv.shape), _const_spec(tw.shape)],
        out_specs=seq,
        out_shape=jax.ShapeDtypeStruct((B, W * mh, HY_LANE), F32),
        compiler_params=_cparams(2),
        name="hy_conv",
    )(v, x1, x2, bias, spec, lhs_r, lhs_rinv, rm, rm_inv, tw)
    return jnp.transpose(out.reshape(B, W, L), (0, 2, 1))


def _swap_perm(width):
    q = width // 4
    return np.concatenate([np.arange(q, 2 * q), np.arange(0, q), np.arange(3 * q, 4 * q), np.arange(2 * q, 3 * q)])


def _rope_tables(L, width, latent):
    if not latent:
        return jnp.ones((L, width), F32), jnp.zeros((L, width), F32)
    q = width // 4
    t = jnp.arange(L, dtype=jnp.int32)
    row = (t // GRID_W).astype(F32)
    col = (t % GRID_W).astype(F32)
    freqs = ROPE_THETA ** (-jnp.arange(q, dtype=F32) / q)
    ar, ac = row[:, None] * freqs[None, :], col[:, None] * freqs[None, :]
    cos = jnp.concatenate([jnp.cos(ar), jnp.cos(ar), jnp.cos(ac), jnp.cos(ac)], axis=-1)
    sin = jnp.concatenate([-jnp.sin(ar), jnp.sin(ar), -jnp.sin(ac), jnp.sin(ac)], axis=-1)
    return cos, sin


def _tables(L, latent):
    ca, sa = _rope_tables(L, MLA_ROPE, latent)
    pad = lambda t, fill: jnp.concatenate(
        [jnp.full((L, MLA_NOPE), fill, F32), t, jnp.zeros((L, MLA_DPAD - MLA_NOPE - MLA_ROPE), F32)], axis=-1)
    cc, sc = _rope_tables(L, GQA_HEAD_DIM, latent)
    twice = lambda t: jnp.concatenate([t, t], axis=-1)
    return pad(ca, 1.0), pad(sa, 0.0), twice(cc), twice(sc)


def _prep_layer_weights(w_in, w_uq, w_ukv, gmq, gmkv, gla_w_gate, gla_b_gate, gcq, gck):
    D = w_in.shape[0]
    col = lambda i: w_in[:, IN_OFFS[i]:IN_OFFS[i + 1]]
    a_cq, a_ckv, a_kr, b_q, b_k, b_v, b_lr, b_og, c_q, c_k, c_v, d_p, gates = [col(i) for i in range(13)]
    p32, p64 = _swap_perm(MLA_ROPE), _swap_perm(GQA_HEAD_DIM)
    z = lambda n: jnp.zeros((D, n), F32)
    pad_kr = lambda t: jnp.concatenate([z(MLA_NOPE), t, z(MLA_DPAD - MLA_NOPE - MLA_ROPE)], axis=1)
    swap_heads = lambda t, n: t.reshape(D, n, GQA_HEAD_DIM)[:, :, p64].reshape(D, n * GQA_HEAD_DIM)
    w_att = jnp.concatenate([a_cq, a_ckv, pad_kr(a_kr), pad_kr(a_kr[:, p32]),
                             c_q, swap_heads(c_q, GQA_HEADS), c_k, swap_heads(c_k, GQA_KV_HEADS), c_v], axis=1)
    w_mix = jnp.concatenate([b_q, b_k, b_v, b_og, b_lr, z(128 - 2 * GLA_GATE_RANK), d_p, gates], axis=1)

    uq = w_uq.reshape(MLA_Q_LORA, MLA_HEADS, MLA_NOPE + MLA_ROPE)
    zq = jnp.zeros((MLA_Q_LORA, MLA_HEADS, MLA_DPAD - MLA_NOPE - MLA_ROPE), F32)
    uq_pad = jnp.concatenate([uq, zq], axis=-1).reshape(MLA_Q_LORA, MLA_HEADS * MLA_DPAD)
    uq_sw = jnp.concatenate([uq[..., :MLA_NOPE], uq[..., MLA_NOPE:][..., p32], zq], axis=-1)
    wuq2 = jnp.concatenate([uq_pad, uq_sw.reshape(MLA_Q_LORA, MLA_HEADS * MLA_DPAD)], axis=1)
    ukv = w_ukv.reshape(MLA_KV_LORA, MLA_HEADS, MLA_NOPE + MLA_V)
    zk = jnp.zeros((MLA_KV_LORA, MLA_HEADS, MLA_DPAD - MLA_NOPE), F32)
    uk = jnp.concatenate([ukv[..., :MLA_NOPE], zk], axis=-1).reshape(MLA_KV_LORA, MLA_HEADS * MLA_DPAD)
    uv = ukv[..., MLA_NOPE:].reshape(MLA_KV_LORA, MLA_HEADS * MLA_V)
    wukv2 = jnp.concatenate([uk, uv], axis=1)

    hk = GLA_HEADS * GLA_DK
    wgate = jnp.zeros((128, 2 * hk), F32)
    wgate = wgate.at[:GLA_GATE_RANK, :hk].set(gla_w_gate[0]).at[GLA_GATE_RANK:2 * GLA_GATE_RANK, hk:].set(gla_w_gate[1])
    return dict(w_att=w_att.astype(BF16), w_mix=w_mix.astype(BF16), wuq2=wuq2.astype(BF16), wukv2=wukv2.astype(BF16),
                gmq=gmq.reshape(1, -1), gmkv=gmkv.reshape(1, -1),
                gcq=jnp.tile(jnp.stack([gcq, gcq[p64]]), (1, 2)), gck=jnp.tile(jnp.stack([gck, gck[p64]]), (1, 2)),
                ones_bd=jnp.asarray(np.kron(np.eye(GQA_HEADS), np.ones((GQA_HEAD_DIM, GQA_HEAD_DIM))), BF16),
                wgate=wgate.astype(BF16), bgate=gla_b_gate.reshape(1, 2 * hk))


def kernel(x, c, ctx, c_ctx, ada_w, ada_b, norm_g, ffn_w_gate, ffn_w_up, ffn_w_down, w_in, mla_q_norm_g, mla_w_uq, mla_kv_norm_g, mla_w_ukv, gla_w_gate, gla_b_gate, gla_norm_g, gqa_q_norm_g, gqa_k_norm_g, hy_sconv_w, hy_sconv_b, hy_filt_w1, hy_filt_b1, hy_filt_w2, hy_filt_b2, hy_filt_w3, hy_filt_bias, w_branch, w_out, final_g):
    B, L, D = x.shape
    Lc = ctx.shape[1]
    depth = ada_w.shape[0]
    tabs_lat = _tables(L, True)
    tabs_ctx = _tables(Lc, False)
    cvec = jnp.concatenate([c, c_ctx[None]], axis=0)
    xc = ctx
    for l in range(depth):
        need_ctx = l < depth - 1
        mod = _ada(cvec, ada_w[l], ada_b[l]).reshape(B + 1, N_ADA, D)
        mod_lat = mod[:B]
        mod_ctx = jnp.broadcast_to(mod[B:], (B, N_ADA, D))
        wg, wu, wd = ffn_w_gate[l].astype(BF16), ffn_w_up[l].astype(BF16), ffn_w_down[l].astype(BF16)
        wp = _prep_layer_weights(w_in[l], mla_w_uq[l], mla_w_ukv[l], mla_q_norm_g[l], mla_kv_norm_g[l],
                                 gla_w_gate[l], gla_b_gate[l], gqa_q_norm_g[l], gqa_k_norm_g[l])
        wb, wo = w_branch[l].astype(BF16), w_out[l].astype(BF16)

        x = _ffn(x, mod_lat[:, 0:3], norm_g[l, 0], wg[0], wu[0], wd[0])
        xc = _ffn(xc, mod_ctx[:, 0:3], norm_g[l, 0], wg[0], wu[0], wd[0])

        qa, ka, va, qc, kc, vc = _attn_prep(x, mod_lat[:, 3:6], norm_g[l, 1], wp, tabs_lat)
        qa_c, ka_c, va_c, qc_c, kc_c, vc_c = _attn_prep(xc, mod_ctx[:, 3:6], norm_g[l, 1], wp, tabs_ctx)
        gq, gk, gv, la, og, dp, gates = _mix_prep(x, mod_lat[:, 3:6], norm_g[l, 1], wp)
        gq_c, gk_c, gv_c, la_c, og_c, dp_c, gates_c = _mix_prep(xc, mod_ctx[:, 3:6], norm_g[l, 1], wp)

        cat = lambda a, b: jnp.concatenate([a, b], axis=2)
        o_a = _attend(qa, cat(ka, ka_c), cat(va, va_c))
        qc_g = qc.reshape(B, GQA_KV_HEADS, (GQA_HEADS // GQA_KV_HEADS) * L, GQA_HEAD_DIM)
        o_c = _attend(qc_g, cat(kc, kc_c), cat(vc, vc_c)).reshape(B, GQA_HEADS, L, GQA_HEAD_DIM)

        s0 = jnp.zeros((B, GLA_HEADS, GLA_DV, GLA_DK), F32)
        ob_c, s_f, s_b = _gla(gq_c, gk_c, gv_c, la_c, s0, s0)
        o_b, _, _ = _gla(gq, gk, gv, la, s_f, s_b)

        hy = (hy_sconv_w[l], hy_sconv_b[l], hy_filt_w1[l], hy_filt_b1[l], hy_filt_w2[l], hy_filt_b2[l],
              hy_filt_w3[l], hy_filt_bias[l])
        o_d = _hyena(dp, *hy)

        x = _merge(x, mod_lat[:, 5:6], o_a, o_b, og, gla_norm_g[l], o_c, o_d, gates, wb, wo)
        last = l == depth - 1
        x = _ffn(x, mod_lat[:, 6:9], norm_g[l, 2], wg[1], wu[1], wd[1], final_g=final_g if last else None)
        if need_ctx:
            o_ac = _attend(qa_c, ka_c, va_c)
            qcc_g = qc_c.reshape(B, GQA_KV_HEADS, (GQA_HEADS // GQA_KV_HEADS) * Lc, GQA_HEAD_DIM)
            o_cc = _attend(qcc_g, kc_c, vc_c).reshape(B, GQA_HEADS, Lc, GQA_HEAD_DIM)
            o_dc = _hyena(dp_c, *hy)
            xc = _merge(xc, mod_ctx[:, 5:6], o_ac, ob_c, og_c, gla_norm_g[l], o_cc, o_dc, gates_c, wb, wo)
            xc = _ffn(xc, mod_ctx[:, 6:9], norm_g[l, 2], wg[1], wu[1], wd[1])
    return x
```

```python
import functools
import math

import jax
import jax.numpy as jnp
import numpy as np
from jax import lax
from jax.experimental import pallas as pl
from jax.experimental.pallas import tpu as pltpu

F32 = jnp.float32
BF16 = jnp.bfloat16

D_MODEL = 1024
GRID_W = 64
N_ADA = 9
FFN_DIM = 2816
RMS_EPS = 1e-6
ROPE_THETA = 10000.0
MLA_HEADS = 8
MLA_NOPE = 64
MLA_ROPE = 32
MLA_V = 64
MLA_Q_LORA = 256
MLA_KV_LORA = 128
MLA_SCALE = (MLA_NOPE + MLA_ROPE) ** -0.5
MLA_DPAD = 128
GLA_HEADS = 4
GLA_DK = 64
GLA_DV = 128
GLA_GATE_RANK = 16
GLA_GATE_TEMP = 16.0
GLA_CHUNK = 64
GQA_HEADS = 8
GQA_KV_HEADS = 2
GQA_HEAD_DIM = 64
GQA_SCALE = GQA_HEAD_DIM ** -0.5
HY_WIDTH = 512
HY_ORDER = 2
HY_BANDS = 16
HY_SIN_FREQ = 1.0
HY_DECAY_TARGET = 1e-2
HY_SHORT_DECAY_PCT = 0.3
HY_LONG_DECAY_PCT = 1.5
N_BRANCH = 4
BRANCH_WIDTH = 512
LOG2E = math.log2(math.e)

IN_SIZES = (MLA_Q_LORA, MLA_KV_LORA, MLA_ROPE,
            GLA_HEADS * GLA_DK, GLA_HEADS * GLA_DK, GLA_HEADS * GLA_DV, 2 * GLA_GATE_RANK, GLA_HEADS * GLA_DV,
            GQA_HEADS * GQA_HEAD_DIM, GQA_KV_HEADS * GQA_HEAD_DIM, GQA_KV_HEADS * GQA_HEAD_DIM,
            (HY_ORDER + 1) * HY_WIDTH,
            N_BRANCH * D_MODEL)
IN_OFFS = tuple(int(v) for v in np.cumsum((0,) + IN_SIZES))

VMEM_LIMIT_BYTES = 56 * 1024 * 1024


def _cparams(n_axes):
    return pltpu.CompilerParams(dimension_semantics=("arbitrary",) * n_axes,
                                vmem_limit_bytes=VMEM_LIMIT_BYTES)


def _const_spec(shape):
    nd = len(shape)
    return pl.BlockSpec(shape, lambda *_: (0,) * nd, pipeline_mode=pl.Buffered(1))


def _pick_tile(n, pref):
    t = min(n, pref)
    while n % t:
        t //= 2
    return t


def _rms(x):
    return x * lax.rsqrt(jnp.mean(x * x, axis=-1, keepdims=True) + RMS_EPS)


def _norm_mod(x, g, shift, scale):
    return (_rms(x) * g) * (1.0 + scale) + shift


def _ada_kernel(s_ref, w_ref, b_ref, o_ref):
    s = s_ref[...]
    s = s * jax.nn.sigmoid(s)
    o_ref[...] = jnp.dot(s.astype(BF16), w_ref[...].astype(BF16), preferred_element_type=F32) + b_ref[...]


def _ada(cvec, w, b):
    rows, d = cvec.shape
    n = w.shape[1]
    tn = _pick_tile(n, 1152)
    return pl.pallas_call(
        _ada_kernel,
        grid=(n // tn,),
        in_specs=[pl.BlockSpec((rows, d), lambda j: (0, 0)),
                  pl.BlockSpec((d, tn), lambda j: (0, j)),
                  pl.BlockSpec((1, tn), lambda j: (0, j))],
        out_specs=pl.BlockSpec((rows, tn), lambda j: (0, j)),
        out_shape=jax.ShapeDtypeStruct((rows, n), F32),
        compiler_params=_cparams(1),
        name="ada_mod",
    )(cvec, w, b.reshape(1, n))


def _ffn_kernel(x_ref, mod_ref, g_ref, wg_ref, wu_ref, wd_ref, fg_ref, o_ref, *, final_norm):
    x = x_ref[0]
    mod = mod_ref[0]
    h = _norm_mod(x, g_ref[...], mod[0:1], mod[1:2]).astype(BF16)
    a = jnp.dot(h, wg_ref[...], preferred_element_type=F32)
    b = jnp.dot(h, wu_ref[...], preferred_element_type=F32)
    act = (a * jax.nn.sigmoid(a) * b).astype(BF16)
    y = jnp.dot(act, wd_ref[...], preferred_element_type=F32)
    out = x + (0.5 * mod[2:3]) * y
    if final_norm:
        out = _rms(out) * fg_ref[...]
    o_ref[0] = out


def _ffn(x, mod3, g, wg, wu, wd, final_g=None):
    B, L, D = x.shape
    F = wg.shape[1]
    tm = _pick_tile(L, 512)
    final_norm = final_g is not None
    fg = (final_g if final_norm else jnp.ones((D,), F32)).reshape(1, D)
    return pl.pallas_call(
        functools.partial(_ffn_kernel, final_norm=final_norm),
        grid=(B, L // tm),
        in_specs=[pl.BlockSpec((1, tm, D), lambda b, i: (b, i, 0)),
                  pl.BlockSpec((1, 3, D), lambda b, i: (b, 0, 0)),
                  _const_spec((1, D)), _const_spec((D, F)), _const_spec((D, F)), _const_spec((F, D)),
                  _const_spec((1, D))],
        out_specs=pl.BlockSpec((1, tm, D), lambda b, i: (b, i, 0)),
        out_shape=jax.ShapeDtypeStruct((B, L, D), F32),
        compiler_params=_cparams(2),
        name="ffn",
    )(x, mod3, g.reshape(1, D), wg, wu, wd, fg)


QA_SCALE = MLA_SCALE * LOG2E
QC_SCALE = GQA_SCALE * LOG2E
_A_CQ, _A_CKV, _A_KR, _A_KRS, _C_Q, _C_QS, _C_K, _C_KS, _C_V, _ATT_COLS = (
    0, 256, 384, 512, 640, 1152, 1664, 1792, 1920, 2048)


def _attn_prep_kernel(x_ref, mod_ref, g_ref, w_ref, wuq_ref, wukv_ref, gmq_ref, gmkv_ref, gcq_ref, gck_ref, ones_ref,
                      cosa_ref, sina_ref, cosc_ref, sinc_ref,
                      qa_ref, ka_ref, va_ref, qc_ref, kc_ref, vc_ref):
    x = x_ref[0]
    mod = mod_ref[0]
    h = _norm_mod(x, g_ref[...], mod[0:1], mod[1:2]).astype(BF16)
    p = jnp.dot(h, w_ref[...], preferred_element_type=F32)
    cosa, sina = cosa_ref[...], sina_ref[...]
    cosc, sinc = cosc_ref[...], sinc_ref[...]

    kr = p[:, _A_KR:_A_KR + 128] * cosa + p[:, _A_KRS:_A_KRS + 128] * sina
    nq = (_rms(p[:, _A_CQ:_A_CQ + MLA_Q_LORA]) * gmq_ref[...]).astype(BF16)
    q2 = jnp.dot(nq, wuq_ref[...], preferred_element_type=F32)
    nkv = (_rms(p[:, _A_CKV:_A_CKV + MLA_KV_LORA]) * gmkv_ref[...]).astype(BF16)
    kv = jnp.dot(nkv, wukv_ref[...], preferred_element_type=F32)
    hq = MLA_HEADS * MLA_DPAD
    for hd in range(MLA_HEADS):
        lo = hd * MLA_DPAD
        qh = q2[:, lo:lo + MLA_DPAD] * cosa + q2[:, hq + lo:hq + lo + MLA_DPAD] * sina
        qa_ref[0, hd] = (qh * QA_SCALE).astype(BF16)
        ka_ref[0, hd] = (kv[:, lo:lo + MLA_DPAD] + kr).astype(BF16)
        va_ref[0, hd] = kv[:, hq + hd * MLA_V:hq + (hd + 1) * MLA_V].astype(BF16)

    def normed_rot(t, ts, gains, ones_bd, scale):
        sq = t * t
        hi = sq.astype(BF16)
        lo = (sq - hi.astype(F32)).astype(BF16)
        ssum = jnp.dot(hi, ones_bd, preferred_element_type=F32) + jnp.dot(lo, ones_bd, preferred_element_type=F32)
        inv = lax.rsqrt(ssum * (1.0 / GQA_HEAD_DIM) + RMS_EPS)
        reps = t.shape[1] // ATT_LANE
        gc = jnp.concatenate([(gains[0:1] * scale) * cosc] * reps, axis=1)
        gs = jnp.concatenate([(gains[1:2] * scale) * sinc] * reps, axis=1)
        return inv * (t * gc + ts * gs)

    dh = GQA_HEAD_DIM
    wq, wk = GQA_HEADS * dh, GQA_KV_HEADS * dh
    q_all = normed_rot(p[:, _C_Q:_C_Q + wq], p[:, _C_QS:_C_QS + wq], gcq_ref[...], ones_ref[...], QC_SCALE)
    k_all = normed_rot(p[:, _C_K:_C_K + wk], p[:, _C_KS:_C_KS + wk], gck_ref[...], ones_ref[:wk, :wk], 1.0)
    for hd in range(GQA_HEADS):
        qc_ref[0, hd] = q_all[:, hd * dh:(hd + 1) * dh].astype(BF16)
    for hd in range(GQA_KV_HEADS):
        kc_ref[0, hd] = k_all[:, hd * dh:(hd + 1) * dh].astype(BF16)
        vc_ref[0, hd] = p[:, _C_V + hd * dh:_C_V + (hd + 1) * dh].astype(BF16)


def _attn_prep(x, mod3, g, wp, tabs):
    B, L, D = x.shape
    tm = _pick_tile(L, 512)
    cosa, sina, cosc, sinc = tabs
    tok = lambda w: pl.BlockSpec((1, tm, w), lambda b, i: (b, i, 0))
    heads = lambda n, w: pl.BlockSpec((1, n, tm, w), lambda b, i: (b, 0, i, 0))
    tab = lambda w: pl.BlockSpec((tm, w), lambda b, i: (i, 0))
    out_shape = (jax.ShapeDtypeStruct((B, MLA_HEADS, L, MLA_DPAD), BF16),
                 jax.ShapeDtypeStruct((B, MLA_HEADS, L, MLA_DPAD), BF16),
                 jax.ShapeDtypeStruct((B, MLA_HEADS, L, MLA_V), BF16),
                 jax.ShapeDtypeStruct((B, GQA_HEADS, L, GQA_HEAD_DIM), BF16),
                 jax.ShapeDtypeStruct((B, GQA_KV_HEADS, L, GQA_HEAD_DIM), BF16),
                 jax.ShapeDtypeStruct((B, GQA_KV_HEADS, L, GQA_HEAD_DIM), BF16))
    return pl.pallas_call(
        _attn_prep_kernel,
        grid=(B, L // tm),
        in_specs=[tok(D), pl.BlockSpec((1, 3, D), lambda b, i: (b, 0, 0)), _const_spec((1, D)),
                  _const_spec(wp["w_att"].shape), _const_spec(wp["wuq2"].shape), _const_spec(wp["wukv2"].shape),
                  _const_spec((1, MLA_Q_LORA)), _const_spec((1, MLA_KV_LORA)),
                  _const_spec((2, ATT_LANE)), _const_spec((2, ATT_LANE)), _const_spec(wp["ones_bd"].shape),
                  tab(MLA_DPAD), tab(MLA_DPAD), tab(ATT_LANE), tab(ATT_LANE)],
        out_specs=(heads(MLA_HEADS, MLA_DPAD), heads(MLA_HEADS, MLA_DPAD), heads(MLA_HEADS, MLA_V),
                   heads(GQA_HEADS, GQA_HEAD_DIM), heads(GQA_KV_HEADS, GQA_HEAD_DIM),
                   heads(GQA_KV_HEADS, GQA_HEAD_DIM)),
        out_shape=out_shape,
        compiler_params=_cparams(2),
        name="attn_prep",
    )(x, mod3, g.reshape(1, D), wp["w_att"], wp["wuq2"], wp["wukv2"], wp["gmq"], wp["gmkv"], wp["gcq"], wp["gck"],
      wp["ones_bd"], cosa, sina, cosc, sinc)


_B_Q, _B_K, _B_V, _B_OG, _B_LR, _D_P, _GATES, _MIX_COLS = (0, 256, 512, 1024, 1536, 1664, 3200, 7296)


def _mix_prep_kernel(x_ref, mod_ref, g_ref, w_ref, wgate_ref, bgate_ref,
                     gq_ref, gk_ref, gv_ref, la_ref, og_ref, dp_ref, gates_ref):
    x = x_ref[0]
    mod = mod_ref[0]
    h = _norm_mod(x, g_ref[...], mod[0:1], mod[1:2]).astype(BF16)
    pg = jnp.dot(h, w_ref[:, _B_Q:_D_P], preferred_element_type=F32)
    gq_ref[0] = pg[:, _B_Q:_B_K] * (GLA_DK ** -0.5)
    gk_ref[0] = pg[:, _B_K:_B_V]
    gv_ref[0] = pg[:, _B_V:_B_OG]
    og_ref[0] = pg[:, _B_OG:_B_LR]
    pre = jnp.dot(pg[:, _B_LR:_D_P].astype(BF16), wgate_ref[...], preferred_element_type=F32) + bgate_ref[...]
    la = (jnp.minimum(pre, 0.0) - jnp.log1p(jnp.exp(-jnp.abs(pre)))) * (1.0 / GLA_GATE_TEMP)
    C = GLA_CHUNK
    ii = lax.broadcasted_iota(jnp.int32, (C, C), 0)
    jj = lax.broadcasted_iota(jnp.int32, (C, C), 1)
    tri_low = jnp.where(ii >= jj, 1.0, 0.0).astype(BF16)
    tri_up = jnp.where(ii <= jj, 1.0, 0.0).astype(BF16)
    hi = la.astype(BF16)
    lo = (la - hi.astype(F32)).astype(BF16)
    hk = GLA_HEADS * GLA_DK
    for c in range(la.shape[0] // C):
        rs = slice(c * C, (c + 1) * C)
        la_ref[0, rs, :hk] = (jnp.dot(tri_low, hi[rs, :hk], preferred_element_type=F32)
                              + jnp.dot(tri_low, lo[rs, :hk], preferred_element_type=F32))
        la_ref[0, rs, hk:] = (jnp.dot(tri_up, hi[rs, hk:], preferred_element_type=F32)
                              + jnp.dot(tri_up, lo[rs, hk:], preferred_element_type=F32))
    dp_ref[0] = jnp.dot(h, w_ref[:, _D_P:_GATES], preferred_element_type=F32)
    gates_ref[0] = jax.nn.sigmoid(jnp.dot(h, w_ref[:, _GATES:_MIX_COLS], preferred_element_type=F32)).astype(BF16)


def _mix_prep(x, mod3, g, wp):
    B, L, D = x.shape
    tm = _pick_tile(L, 256)
    tok = lambda w: pl.BlockSpec((1, tm, w), lambda b, i: (b, i, 0))
    widths = (256, 256, 512, 512, 512, 1536, 4096)
    dtypes = (F32, F32, F32, F32, F32, F32, BF16)
    return pl.pallas_call(
        _mix_prep_kernel,
        grid=(B, L // tm),
        in_specs=[tok(D), pl.BlockSpec((1, 3, D), lambda b, i: (b, 0, 0)), _const_spec((1, D)),
                  _const_spec(wp["w_mix"].shape), _const_spec(wp["wgate"].shape), _const_spec((1, 512))],
        out_specs=tuple(tok(w) for w in widths),
        out_shape=tuple(jax.ShapeDtypeStruct((B, L, w), dt) for w, dt in zip(widths, dtypes)),
        compiler_params=_cparams(2),
        name="mix_prep",
    )(x, mod3, g.reshape(1, D), wp["w_mix"], wp["wgate"], wp["bgate"])


ATT_LANE = 128
ATT_TK = 256


def _lane_fold(x, op):
    acc = x[:, :ATT_LANE]
    for j in range(1, x.shape[1] // ATT_LANE):
        acc = op(acc, x[:, j * ATT_LANE:(j + 1) * ATT_LANE])
    return acc


def _attn_kernel(q_ref, *refs, tk):
    o_ref = refs[-1]
    q = q_ref[0, 0]
    nt = (((1,), (1,)), ((), ()))
    chunks = [(k_ref, v_ref, j) for k_ref, v_ref in zip(refs[0:-1:2], refs[1:-1:2])
              for j in range(k_ref.shape[2] // tk)]
    m = ls = acc = None
    for c, (k_ref, v_ref, j) in enumerate(chunks):
        s = lax.dot_general(q, k_ref[0, 0, j * tk:(j + 1) * tk, :], nt, preferred_element_type=F32)
        cm = jnp.max(s, axis=-1, keepdims=True)
        m_new = cm if c == 0 else jnp.maximum(m, cm)
        p = jnp.exp2(s - m_new)
        cl = _lane_fold(p, jnp.add)
        pv = jnp.dot(p.astype(BF16), v_ref[0, 0, j * tk:(j + 1) * tk, :], preferred_element_type=F32)
        if c == 0:
            ls, acc = cl, pv
        else:
            alpha = jnp.exp2(m - m_new)
            ls = alpha * ls + cl
            acc = alpha * acc + pv
        m = m_new
    l = jnp.sum(ls, axis=-1, keepdims=True)
    o_ref[0, 0] = (acc / l).astype(o_ref.dtype)


def _attend(q, *kv):
    B, H, Lq, d = q.shape
    dv = kv[1].shape[3]
    tq = _pick_tile(Lq, 1024)
    tk = ATT_TK
    assert all(t.shape[2] % tk == 0 for t in kv)
    seg = lambda t: pl.BlockSpec((1, 1) + t.shape[2:], lambda b, h, i: (b, h, 0, 0))
    return pl.pallas_call(
        functools.partial(_attn_kernel, tk=tk),
        grid=(B, H, Lq // tq),
        in_specs=[pl.BlockSpec((1, 1, tq, d), lambda b, h, i: (b, h, i, 0))] + [seg(t) for t in kv],
        out_specs=pl.BlockSpec((1, 1, tq, dv), lambda b, h, i: (b, h, i, 0)),
        out_shape=jax.ShapeDtypeStruct((B, H, Lq, dv), BF16),
        compiler_params=_cparams(3),
        name="attend",
    )(q, *kv)


def _merge_kernel(x_ref, mod_ref, oa_ref, ob_ref, og_ref, gn_ref, oc_ref, od_ref, gates_ref, wb_ref, wo_ref, o_ref):
    oa = jnp.concatenate([oa_ref[0, hd] for hd in range(MLA_HEADS)], axis=-1)
    oc = jnp.concatenate([oc_ref[0, hd] for hd in range(GQA_HEADS)], axis=-1)
    ob = ob_ref[0]
    ob = jnp.concatenate([_rms(ob[:, hd * GLA_DV:(hd + 1) * GLA_DV]) * gn_ref[...] for hd in range(GLA_HEADS)], axis=-1)
    og = og_ref[0]
    ob = ob * (og * jax.nn.sigmoid(og))
    branches = (oa, ob.astype(BF16), oc, od_ref[0].astype(BF16))
    y = None
    for i, o in enumerate(branches):
        gate = gates_ref[0, :, i * D_MODEL:(i + 1) * D_MODEL].astype(F32)
        t = gate * jnp.dot(o, wb_ref[i], preferred_element_type=F32)
        y = t if y is None else y + t
    z = jnp.dot(y.astype(BF16), wo_ref[...], preferred_element_type=F32)
    o_ref[0] = x_ref[0] + mod_ref[0] * z


def _merge(x, g2, oa, ob, og, gla_g, oc, od, gates, wb, wo):
    B, L, D = x.shape
    tm = _pick_tile(L, 512)
    tok = lambda w: pl.BlockSpec((1, tm, w), lambda b, i: (b, i, 0))
    heads = lambda n, w: pl.BlockSpec((1, n, tm, w), lambda b, i: (b, 0, i, 0))
    return pl.pallas_call(
        _merge_kernel,
        grid=(B, L // tm),
        in_specs=[tok(D), pl.BlockSpec((1, 1, D), lambda b, i: (b, 0, 0)),
                  heads(MLA_HEADS, MLA_V), tok(BRANCH_WIDTH), tok(BRANCH_WIDTH), _const_spec((1, GLA_DV)),
                  heads(GQA_HEADS, GQA_HEAD_DIM), tok(BRANCH_WIDTH),
                  tok(N_BRANCH * D), _const_spec(wb.shape), _const_spec(wo.shape)],
        out_specs=tok(D),
        out_shape=jax.ShapeDtypeStruct((B, L, D), F32),
        compiler_params=_cparams(2),
        name="merge",
    )(x, g2, oa, ob, og, gla_g.reshape(1, GLA_DV), oc, od, gates, wb, wo)


GLA_PAIR = 2


def _gla_kernel(q_ref, k_ref, v_ref, laf_ref, lab_ref, sf0_ref, sb0_ref, o_ref, sf_ref, sb_ref, stf, stb, *, n_chunks):
    C = GLA_CHUNK
    ii = lax.broadcasted_iota(jnp.int32, (C, C), 0)
    jj = lax.broadcasted_iota(jnp.int32, (C, C), 1)
    low, up = ii >= jj, ii <= jj
    nt = (((1,), (1,)), ((), ()))
    stf[...] = sf0_ref[0]
    stb[...] = sb0_ref[0]

    def one_direction(n, mask, bc_ref, st, last_row, accumulate):
        rows = pl.ds(pl.multiple_of(n * C, C), C)
        q, k, v, bc = q_ref[0, rows, :], k_ref[0, rows, :], v_ref[0, rows, :], bc_ref[0, rows, :]
        b_last = bc[last_row:last_row + 1, :]
        q_dec = (q * jnp.exp(bc)).astype(BF16)
        k_inv = (k * jnp.exp(-bc)).astype(BF16)
        k_end = (k * jnp.exp(b_last - bc)).astype(BF16)
        decay = jnp.exp(b_last)
        outs = []
        for h in range(GLA_PAIR):
            ks = slice(h * GLA_DK, (h + 1) * GLA_DK)
            vh = v[:, h * GLA_DV:(h + 1) * GLA_DV]
            a = lax.dot_general(q_dec[:, ks], k_inv[:, ks], nt, preferred_element_type=F32)
            a = jnp.where(mask, a, 0.0).astype(BF16)
            s_t = st[h]
            outs.append(jnp.dot(a, vh.astype(BF16), preferred_element_type=F32)
                        + lax.dot_general(q_dec[:, ks], s_t.astype(BF16), nt, preferred_element_type=F32))
            st[h] = s_t * decay[:, ks] + jnp.dot(vh.T.astype(BF16), k_end[:, ks], preferred_element_type=F32)
        o = jnp.concatenate(outs, axis=-1)
        if accumulate:
            o_ref[0, rows, :] = o_ref[0, rows, :] + o
        else:
            o_ref[0, rows, :] = o

    def sweep(accumulate):
        def body(n, carry):
            one_direction(n, low, laf_ref, stf, C - 1, accumulate)
            one_direction(n_chunks - 1 - n, up, lab_ref, stb, 0, accumulate)
            return carry
        return body

    half = n_chunks // 2
    lax.fori_loop(0, half, sweep(False), 0, unroll=2)
    lax.fori_loop(half, n_chunks, sweep(True), 0, unroll=2)
    sf_ref[0] = stf[...]
    sb_ref[0] = stb[...]


def _gla(q, k, v, la, sf0, sb0):
    B, L, _ = q.shape
    n_chunks = L // GLA_CHUNK
    assert L % GLA_CHUNK == 0 and n_chunks % 2 == 0
    wk, wv, npair = GLA_PAIR * GLA_DK, GLA_PAIR * GLA_DV, GLA_HEADS // GLA_PAIR
    seq = lambda w, off: pl.BlockSpec((1, L, w), lambda b, p: (b, 0, p + off))
    st_spec = pl.BlockSpec((1, GLA_PAIR, GLA_DV, GLA_DK), lambda b, p: (b, p, 0, 0))
    st_shape = jax.ShapeDtypeStruct((B, GLA_HEADS, GLA_DV, GLA_DK), F32)
    return pl.pallas_call(
        functools.partial(_gla_kernel, n_chunks=n_chunks),
        grid=(B, npair),
        in_specs=[seq(wk, 0), seq(wk, 0), seq(wv, 0), seq(wk, 0), seq(wk, npair), st_spec, st_spec],
        out_specs=(seq(wv, 0), st_spec, st_spec),
        out_shape=(jax.ShapeDtypeStruct((B, L, GLA_HEADS * GLA_DV), F32), st_shape, st_shape),
        scratch_shapes=[pltpu.VMEM((GLA_PAIR, GLA_DV, GLA_DK), F32), pltpu.VMEM((GLA_PAIR, GLA_DV, GLA_DK), F32)],
        compiler_params=_cparams(2),
        name="gla",
    )(q, k, v, la, la, sf0, sb0)


HY_LANE = 128


def _sconv_kernel(p1_ref, p2_ref, p3_ref, w1_ref, w2_ref, w3_ref, b1_ref, b2_ref, b3_ref, x1_ref, x2_ref, v_ref):
    L = p1_ref.shape[1]
    row = lax.broadcasted_iota(jnp.int32, (L, HY_LANE), 0)
    for p_ref, w_ref, b_ref, o_ref in ((p1_ref, w1_ref, b1_ref, x1_ref), (p2_ref, w2_ref, b2_ref, x2_ref),
                                       (p3_ref, w3_ref, b3_ref, v_ref)):
        x = p_ref[0]
        prev = jnp.where(row == 0, 0.0, pltpu.roll(x, 1, 0))
        nxt = jnp.where(row == L - 1, 0.0, pltpu.roll(x, L - 1, 0))
        o_ref[0] = prev * w_ref[0:1, :] + x * w_ref[1:2, :] + nxt * w_ref[2:3, :] + b_ref[...]


def _sconv(p, w, b):
    B, L, _ = p.shape
    nb = HY_WIDTH // HY_LANE
    pspec = lambda g: pl.BlockSpec((1, L, HY_LANE), lambda bb, j: (bb, 0, g * nb + j))
    wspec = lambda g: pl.BlockSpec((3, HY_LANE), lambda bb, j: (0, g * nb + j))
    bspec = lambda g: pl.BlockSpec((1, HY_LANE), lambda bb, j: (0, g * nb + j))
    ospec = pl.BlockSpec((1, L, HY_LANE), lambda bb, j: (bb, 0, j))
    oshape = jax.ShapeDtypeStruct((B, L, HY_WIDTH), F32)
    b2 = b.reshape(1, -1)
    return pl.pallas_call(
        _sconv_kernel,
        grid=(B, nb),
        in_specs=[pspec(0), pspec(1), pspec(2), wspec(0), wspec(1), wspec(2), bspec(0), bspec(1), bspec(2)],
        out_specs=(ospec, ospec, ospec),
        out_shape=(oshape, oshape, oshape),
        compiler_params=_cparams(2),
        name="hy_sconv",
    )(p, p, p, w, w, w, b2, b2, b2)


def _hy_filter_kernel(feat_ref, w1_ref, b1_ref, w2_ref, b2_ref, w3_ref, delta_ref, o_ref):
    hp = lax.Precision.HIGHEST
    feat = feat_ref[...]
    h = jnp.sin(HY_SIN_FREQ * (jnp.dot(feat, w1_ref[...], precision=hp, preferred_element_type=F32) + b1_ref[...]))
    h = jnp.sin(HY_SIN_FREQ * (jnp.dot(h, w2_ref[...], precision=hp, preferred_element_type=F32) + b2_ref[...]))
    h = jnp.dot(h, w3_ref[...], precision=hp, preferred_element_type=F32)
    win = jnp.exp(-feat[:, 0:1] * delta_ref[...])
    lag0 = (pl.program_id(0) == 0) & (lax.broadcasted_iota(jnp.int32, win.shape, 0) == 0)
    win_b = jnp.where(lag0, 0.0, win)
    for o in range(HY_ORDER):
        for d, wnd in enumerate((win, win_b)):
            lo = (o * 2 + d) * HY_WIDTH
            o_ref[0, :, lo:lo + HY_WIDTH] = h[:, lo:lo + HY_WIDTH] * wnd


def _hy_filters(L, w1, b1, w2, b2, w3):
    feat, deltas = _hy_feat_consts(L)
    emb = w1.shape[0]
    w1p = jnp.zeros((HY_LANE, w1.shape[1]), F32).at[:emb].set(w1)
    tl = _pick_tile(L, 512)
    nf = w3.shape[1]
    return pl.pallas_call(
        _hy_filter_kernel,
        grid=(L // tl,),
        in_specs=[pl.BlockSpec((tl, HY_LANE), lambda i: (i, 0)), _const_spec(w1p.shape), _const_spec((1, w1.shape[1])),
                  _const_spec(w2.shape), _const_spec((1, w2.shape[1])), _const_spec(w3.shape),
                  _const_spec((1, HY_WIDTH))],
        out_specs=pl.BlockSpec((1, tl, nf), lambda i: (0, i, 0)),
        out_shape=jax.ShapeDtypeStruct((1, L, nf), F32),
        compiler_params=_cparams(1),
        name="hy_filter",
    )(jnp.asarray(feat), w1p, b1.reshape(1, -1), w2, b2.reshape(1, -1), w3, jnp.asarray(deltas))


@functools.lru_cache(maxsize=None)
def _hy_feat_consts(L):
    t = np.linspace(0.0, 1.0, L)
    w = 2.0 * np.pi * np.arange(L) / L
    f = np.linspace(1e-4, HY_BANDS - 1, HY_BANDS)
    fw = w[:, None] * f[None, :]
    feat = np.zeros((L, HY_LANE), np.float32)
    feat[:, :2 * HY_BANDS + 1] = np.concatenate([t[:, None], np.cos(fw), -np.sin(fw)], axis=-1)
    deltas = np.abs(np.linspace(math.log(HY_DECAY_TARGET) / HY_LONG_DECAY_PCT,
                                math.log(HY_DECAY_TARGET) / HY_SHORT_DECAY_PCT, HY_WIDTH))
    return feat, deltas.reshape(1, HY_WIDTH).astype(np.float32)


HY_GROUP_ROWS = 256


@functools.lru_cache(maxsize=None)
def _cm_consts(L):
    assert L % HY_LANE == 0 and HY_GROUP_ROWS % (L // HY_LANE) == 0
    N = 2 * L
    mh = L // HY_LANE
    m2n = 2 * mh
    g = HY_GROUP_ROWS // mh
    kb, m2 = np.arange(m2n), np.arange(mh)
    c = np.exp(-2j * np.pi * np.outer(kb, m2) / m2n)
    eye = np.eye(g)
    lhs_r = np.concatenate([np.kron(eye, c.real), np.kron(eye, c.imag)], axis=0)
    m1 = np.arange(HY_LANE)
    tw = np.tile(np.exp(-2j * np.pi * np.outer(kb, m1) / N), (g, 1))
    f = np.exp(-2j * np.pi * np.outer(m1, m1) / HY_LANE)
    rm = np.block([[f.real, f.imag], [-f.imag, f.real]])
    rm_inv = np.block([[f.real, -f.imag], [f.imag, f.real]])
    gi = np.exp(2j * np.pi * np.outer(m2, kb) / m2n) / N
    lhs_rinv = np.concatenate([np.kron(eye, gi.real), -np.kron(eye, gi.imag)], axis=1)
    f32 = lambda m: np.ascontiguousarray(m, dtype=np.float32)
    return dict(mh=mh, m2n=m2n, g=g, lhs_r=f32(lhs_r), lhs_rinv=f32(lhs_rinv), rm=f32(rm), rm_inv=f32(rm_inv),
                tw=f32(np.stack([tw.real, tw.imag])))


def _cm_forward(za, zb, lhs_r, tw_r, tw_i, rm):
    half = lhs_r.shape[0] // 2
    b = jnp.dot(lhs_r, jnp.concatenate([za, zb], axis=1).astype(BF16), preferred_element_type=F32)
    out = []
    for lanes in (slice(0, HY_LANE), slice(HY_LANE, 2 * HY_LANE)):
        br, bi = b[:half, lanes], b[half:, lanes]
        bp = jnp.concatenate([br * tw_r - bi * tw_i, br * tw_i + bi * tw_r], axis=1).astype(BF16)
        out.append(jnp.dot(bp, rm, preferred_element_type=F32))
    return out


def _cm_conv_kernel(v_ref, x1_ref, x2_ref, bias_ref, h_ref, lr_ref, lri_ref, rm_ref, rmi_ref, tw_ref, o_ref, *, groups):
    rows = HY_GROUP_ROWS
    tw_r, tw_i = tw_ref[0], tw_ref[1]
    lhs_r, lhs_rinv, rm, rm_inv = lr_ref[...], lri_ref[...], rm_ref[...], rmi_ref[...]
    hrows = tw_r.shape[0]
    for pair in range(groups // 2):
        rss = [slice(gi * rows, (gi + 1) * rows) for gi in (2 * pair, 2 * pair + 1)]
        hss = [slice(gi * hrows, (gi + 1) * hrows) for gi in (2 * pair, 2 * pair + 1)]
        zs = [v_ref[0, rs, :] for rs in rss]
        for o, gate_ref in enumerate((x1_ref, x2_ref)):
            zps = []
            for x, hs in zip(_cm_forward(zs[0], zs[1], lhs_r, tw_r, tw_i, rm), hss):
                xr, xi = x[:, :HY_LANE], x[:, HY_LANE:]
                hr, hi = h_ref[o, hs, :HY_LANE], h_ref[o, hs, HY_LANE:]
                y = jnp.concatenate([xr * hr - xi * hi, xr * hi + xi * hr], axis=1).astype(BF16)
                zc = jnp.dot(y, rm_inv, preferred_element_type=F32)
                zr, zi = zc[:, :HY_LANE], zc[:, HY_LANE:]
                zps.append(jnp.concatenate([zr * tw_r + zi * tw_i, zi * tw_r - zr * tw_i], axis=0).astype(BF16))
            conv = jnp.dot(lhs_rinv, jnp.concatenate(zps, axis=1), preferred_element_type=F32)
            zs = [gate_ref[0, rs, :] * (conv[:, i * HY_LANE:(i + 1) * HY_LANE] + z * bias_ref[o, rs, :])
                  for i, (z, rs) in enumerate(zip(zs, rss))]
        for z, rs in zip(zs, rss):
            o_ref[0, rs, :] = z


def _cm_spectrum_kernel(f_ref, b_ref, lr_ref, rm_ref, tw_ref, o_ref, *, groups):
    rows, hrows = HY_GROUP_ROWS, tw_ref.shape[1]
    for gi in range(groups):
        rs = slice(gi * rows, (gi + 1) * rows)
        hs = slice(gi * hrows, (gi + 1) * hrows)
        xf, xb = _cm_forward(f_ref[rs, :], b_ref[rs, :], lr_ref[...], tw_ref[0], tw_ref[1], rm_ref[...])
        o_ref[0, hs, :HY_LANE] = xf[:, :HY_LANE] + xb[:, :HY_LANE]
        o_ref[0, hs, HY_LANE:] = xf[:, HY_LANE:] - xb[:, HY_LANE:]


def _to_channel_major(t):
    B, L, C = t.shape
    return jnp.transpose(t, (0, 2, 1)).reshape(B, C * (L // HY_LANE), HY_LANE)


def _hyena(dp, sconv_w, sconv_b, w1, b1, w2, b2, w3, fbias):
    B, L, _ = dp.shape
    W = HY_WIDTH
    cst = _cm_consts(L)
    mh, m2n, g = cst["mh"], cst["m2n"], cst["g"]
    lhs_r, lhs_rinv = jnp.asarray(cst["lhs_r"], BF16), jnp.asarray(cst["lhs_rinv"], BF16)
    rm, rm_inv = jnp.asarray(cst["rm"], BF16), jnp.asarray(cst["rm_inv"], BF16)
    tw = jnp.asarray(cst["tw"])
    rows, hrows = HY_GROUP_ROWS, g * m2n

    filt = _to_channel_major(_hy_filters(L, w1, b1, w2, b2, w3))[0]
    ng = W // g
    groups = min(4, ng)
    assert groups % 2 == 0 and ng % groups == 0
    nb = ng // groups
    spec = pl.pallas_call(
        functools.partial(_cm_spectrum_kernel, groups=groups),
        grid=(HY_ORDER, nb),
        in_specs=[pl.BlockSpec((groups * rows, HY_LANE), lambda o, j: ((2 * o) * nb + j, 0)),
                  pl.BlockSpec((groups * rows, HY_LANE), lambda o, j: ((2 * o + 1) * nb + j, 0)),
                  _const_spec(lhs_r.shape), _const_spec(rm.shape), _const_spec(tw.shape)],
        out_specs=pl.BlockSpec((1, groups * hrows, 2 * HY_LANE), lambda o, j: (o, j, 0)),
        out_shape=jax.ShapeDtypeStruct((HY_ORDER, W * m2n, 2 * HY_LANE), F32),
        compiler_params=_cparams(2),
        name="hy_spectrum",
    )(filt, filt, lhs_r, rm, tw)

    x1, x2, v = _sconv(dp, sconv_w, sconv_b)
    x1, x2, v = _to_channel_major(x1), _to_channel_major(x2), _to_channel_major(v)
    bias = jnp.broadcast_to(jnp.repeat(fbias, mh, axis=1)[:, :, None], (HY_ORDER, W * mh, HY_LANE))
    seq =pl.BlockSpec((1, groups * rows, HY_LANE), lambda j, b: (b, j, 0))
    out = pl.pallas_call(
        functools.partial(_cm_conv_kernel, groups=groups),
        grid=(ng // groups, B),
        in_specs=[seq, seq, seq,
                  pl.BlockSpec((HY_ORDER, groups * rows, HY_LANE), lambda j, b: (0, j, 0)),
                  pl.BlockSpec((HY_ORDER, groups * hrows, 2 * HY_LANE), lambda j, b: (0, j, 0)),
                  _const_spec(lhs_r.shape), _const_spec(lhs_rinv.shape), _const_spec(rm.shape),
                  _const_spec(rm_inv.shape), _const_spec(tw.shape)],
        out_specs=seq,
        out_shape=jax.ShapeDtypeStruct((B, W * mh, HY_LANE), F32),
        compiler_params=_cparams(2),
        name="hy_conv",
    )(v, x1, x2, bias, spec, lhs_r, lhs_rinv, rm, rm_inv, tw)
    return jnp.transpose(out.reshape(B, W, L), (0, 2, 1))


def _swap_perm(width):
    q = width // 4
    return np.concatenate([np.arange(q, 2 * q), np.arange(0, q), np.arange(3 * q, 4 * q), np.arange(2 * q, 3 * q)])


def _rope_tables(L, width, latent):
    if not latent:
        return jnp.ones((L, width), F32), jnp.zeros((L, width), F32)
    q = width // 4
    t = jnp.arange(L, dtype=jnp.int32)
    row = (t // GRID_W).astype(F32)
    col = (t % GRID_W).astype(F32)
    freqs = ROPE_THETA ** (-jnp.arange(q, dtype=F32) / q)
    ar, ac = row[:, None] * freqs[None, :], col[:, None] * freqs[None, :]
    cos = jnp.concatenate([jnp.cos(ar), jnp.cos(ar), jnp.cos(ac), jnp.cos(ac)], axis=-1)
    sin = jnp.concatenate([-jnp.sin(ar), jnp.sin(ar), -jnp.sin(ac), jnp.sin(ac)], axis=-1)
    return cos, sin


def _tables(L, latent):
    ca, sa = _rope_tables(L, MLA_ROPE, latent)
    pad = lambda t, fill: jnp.concatenate(
        [jnp.full((L, MLA_NOPE), fill, F32), t, jnp.zeros((L, MLA_DPAD - MLA_NOPE - MLA_ROPE), F32)], axis=-1)
    cc, sc = _rope_tables(L, GQA_HEAD_DIM, latent)
    twice = lambda t: jnp.concatenate([t, t], axis=-1)
    return pad(ca, 1.0), pad(sa, 0.0), twice(cc), twice(sc)


def _prep_layer_weights(w_in, w_uq, w_ukv, gmq, gmkv, gla_w_gate, gla_b_gate, gcq, gck):
    D = w_in.shape[0]
    col = lambda i: w_in[:, IN_OFFS[i]:IN_OFFS[i + 1]]
    a_cq, a_ckv, a_kr, b_q, b_k, b_v, b_lr, b_og, c_q, c_k, c_v, d_p, gates = [col(i) for i in range(13)]
    p32, p64 = _swap_perm(MLA_ROPE), _swap_perm(GQA_HEAD_DIM)
    z = lambda n: jnp.zeros((D, n), F32)
    pad_kr = lambda t: jnp.concatenate([z(MLA_NOPE), t, z(MLA_DPAD - MLA_NOPE - MLA_ROPE)], axis=1)
    swap_heads = lambda t, n: t.reshape(D, n, GQA_HEAD_DIM)[:, :, p64].reshape(D, n * GQA_HEAD_DIM)
    w_att = jnp.concatenate([a_cq, a_ckv, pad_kr(a_kr), pad_kr(a_kr[:, p32]),
                             c_q, swap_heads(c_q, GQA_HEADS), c_k, swap_heads(c_k, GQA_KV_HEADS), c_v], axis=1)
    w_mix = jnp.concatenate([b_q, b_k, b_v, b_og, b_lr, z(128 - 2 * GLA_GATE_RANK), d_p, gates], axis=1)

    uq = w_uq.reshape(MLA_Q_LORA, MLA_HEADS, MLA_NOPE + MLA_ROPE)
    zq = jnp.zeros((MLA_Q_LORA, MLA_HEADS, MLA_DPAD - MLA_NOPE - MLA_ROPE), F32)
    uq_pad = jnp.concatenate([uq, zq], axis=-1).reshape(MLA_Q_LORA, MLA_HEADS * MLA_DPAD)
    uq_sw = jnp.concatenate([uq[..., :MLA_NOPE], uq[..., MLA_NOPE:][..., p32], zq], axis=-1)
    wuq2 = jnp.concatenate([uq_pad, uq_sw.reshape(MLA_Q_LORA, MLA_HEADS * MLA_DPAD)], axis=1)
    ukv = w_ukv.reshape(MLA_KV_LORA, MLA_HEADS, MLA_NOPE + MLA_V)
    zk = jnp.zeros((MLA_KV_LORA, MLA_HEADS, MLA_DPAD - MLA_NOPE), F32)
    uk = jnp.concatenate([ukv[..., :MLA_NOPE], zk], axis=-1).reshape(MLA_KV_LORA, MLA_HEADS * MLA_DPAD)
    uv = ukv[..., MLA_NOPE:].reshape(MLA_KV_LORA, MLA_HEADS * MLA_V)
    wukv2 = jnp.concatenate([uk, uv], axis=1)

    hk = GLA_HEADS * GLA_DK
    wgate = jnp.zeros((128, 2 * hk), F32)
    wgate = wgate.at[:GLA_GATE_RANK, :hk].set(gla_w_gate[0]).at[GLA_GATE_RANK:2 * GLA_GATE_RANK, hk:].set(gla_w_gate[1])
    return dict(w_att=w_att.astype(BF16), w_mix=w_mix.astype(BF16), wuq2=wuq2.astype(BF16), wukv2=wukv2.astype(BF16),
                gmq=gmq.reshape(1, -1), gmkv=gmkv.reshape(1, -1),
                gcq=jnp.tile(jnp.stack([gcq, gcq[p64]]), (1, 2)), gck=jnp.tile(jnp.stack([gck, gck[p64]]), (1, 2)),
                ones_bd=jnp.asarray(np.kron(np.eye(GQA_HEADS), np.ones((GQA_HEAD_DIM, GQA_HEAD_DIM))), BF16),
                wgate=wgate.astype(BF16), bgate=gla_b_gate.reshape(1, 2 * hk))


def kernel(x, c, ctx, c_ctx, ada_w, ada_b, norm_g, ffn_w_gate, ffn_w_up, ffn_w_down, w_in, mla_q_norm_g, mla_w_uq, mla_kv_norm_g, mla_w_ukv, gla_w_gate, gla_b_gate, gla_norm_g, gqa_q_norm_g, gqa_k_norm_g, hy_sconv_w, hy_sconv_b, hy_filt_w1, hy_filt_b1, hy_filt_w2, hy_filt_b2, hy_filt_w3, hy_filt_bias, w_branch, w_out, final_g):
    B, L, D = x.shape
    Lc = ctx.shape[1]
    depth = ada_w.shape[0]
    tabs_lat = _tables(L, True)
    tabs_ctx = _tables(Lc, False)
    cvec = jnp.concatenate([c, c_ctx[None]], axis=0)
    xc = ctx
    for l in range(depth):
        need_ctx = l < depth - 1
        mod = _ada(cvec, ada_w[l], ada_b[l]).reshape(B + 1, N_ADA, D)
        mod_lat = mod[:B]
        mod_ctx = jnp.broadcast_to(mod[B:], (B, N_ADA, D))
        wg, wu, wd = ffn_w_gate[l].astype(BF16), ffn_w_up[l].astype(BF16), ffn_w_down[l].astype(BF16)
        wp = _prep_layer_weights(w_in[l], mla_w_uq[l], mla_w_ukv[l], mla_q_norm_g[l], mla_kv_norm_g[l],
                                 gla_w_gate[l], gla_b_gate[l], gqa_q_norm_g[l], gqa_k_norm_g[l])
        wb, wo = w_branch[l].astype(BF16), w_out[l].astype(BF16)

        x = _ffn(x, mod_lat[:, 0:3], norm_g[l, 0], wg[0], wu[0], wd[0])
        xc = _ffn(xc, mod_ctx[:, 0:3], norm_g[l, 0], wg[0], wu[0], wd[0])

        qa, ka, va, qc, kc, vc = _attn_prep(x, mod_lat[:, 3:6], norm_g[l, 1], wp, tabs_lat)
        qa_c, ka_c, va_c, qc_c, kc_c, vc_c = _attn_prep(xc, mod_ctx[:, 3:6], norm_g[l, 1], wp, tabs_ctx)
        gq, gk, gv, la, og, dp, gates = _mix_prep(x, mod_lat[:, 3:6], norm_g[l, 1], wp)
        gq_c, gk_c, gv_c, la_c, og_c, dp_c, gates_c = _mix_prep(xc, mod_ctx[:, 3:6], norm_g[l, 1], wp)

        o_a = _attend(qa, ka, va, ka_c, va_c)
        qc_g = qc.reshape(B, GQA_KV_HEADS, (GQA_HEADS // GQA_KV_HEADS) * L, GQA_HEAD_DIM)
        o_c = _attend(qc_g, kc, vc, kc_c, vc_c).reshape(B, GQA_HEADS, L, GQA_HEAD_DIM)

        s0 = jnp.zeros((B, GLA_HEADS, GLA_DV, GLA_DK), F32)
        ob_c, s_f, s_b = _gla(gq_c, gk_c, gv_c, la_c, s0, s0)
        o_b, _, _ = _gla(gq, gk, gv, la, s_f, s_b)

        hy = (hy_sconv_w[l], hy_sconv_b[l], hy_filt_w1[l], hy_filt_b1[l], hy_filt_w2[l], hy_filt_b2[l],
              hy_filt_w3[l], hy_filt_bias[l])
        o_d = _hyena(dp, *hy)

        x = _merge(x, mod_lat[:, 5:6], o_a, o_b, og, gla_norm_g[l], o_c, o_d, gates, wb, wo)
        last = l == depth - 1
        x = _ffn(x, mod_lat[:, 6:9], norm_g[l, 2], wg[1], wu[1], wd[1], final_g=final_g if last else None)
        if need_ctx:
            o_ac = _attend(qa_c, ka_c, va_c)
            qcc_g = qc_c.reshape(B, GQA_KV_HEADS, (GQA_HEADS // GQA_KV_HEADS) * Lc, GQA_HEAD_DIM)
            o_cc = _attend(qcc_g, kc_c, vc_c).reshape(B, GQA_HEADS, Lc, GQA_HEAD_DIM)
            o_dc = _hyena(dp_c, *hy)
            xc = _merge(xc, mod_ctx[:, 5:6], o_ac, ob_c, og_c, gla_norm_g[l], o_cc, o_dc, gates_c, wb, wo)
            xc = _ffn(xc, mod_ctx[:, 6:9], norm_g[l, 2], wg[1], wu[1], wd[1])
    return x
```
